```python
import jax
import jax.numpy as jnp
from jax import lax
import numpy as np

D_MODEL = 4096
BATCH = 4
SEQ = 2048
DEPTH = 4
DEC_BATCH = 8
DEC_SEQ = 4
PAST_LEN = 8192
PAGE_SIZE = 128

N_MIXERS = 2
N_CONV_LAYERS = (DEPTH + 1) // 2
N_ATTN_LAYERS = DEPTH // 2
N_HEADS = 32
HEAD_DIM = D_MODEL // N_HEADS
CONV_WIDTH = 31
D_FF = 256 * ((8 * D_MODEL // 3 + 255) // 256)
FFN_CONV_WIDTH = 3
Q_BLOCK = 128
N_MOD = 6
EPS = 1e-6
NEG_INF = -1e30

kernel_name = "conformer_fox_hybrid_decode_step"


def rmsnorm(x, g):
    xf = x.astype(jnp.float32)
    y = xf * lax.rsqrt(jnp.mean(xf * xf, axis=-1, keepdims=True) + EPS)
    return (y * g.astype(jnp.float32)).astype(x.dtype)


def layernorm(x, g, b):
    xf = x.astype(jnp.float32)
    mu = jnp.mean(xf, axis=-1, keepdims=True)
    var = jnp.mean(jnp.square(xf - mu), axis=-1, keepdims=True)
    y = (xf - mu) * lax.rsqrt(var + EPS) * g.astype(jnp.float32) + b.astype(jnp.float32)
    return y.astype(x.dtype)


def modulate(h, shift, scale):
    return h * (1 + scale[:, None, :]) + shift[:, None, :]


def causal_dwconv(x, prefix, w, b):
    xp = jnp.concatenate([prefix.astype(x.dtype), x], axis=1)
    y = lax.conv_general_dilated(xp, w[:, None, :].astype(x.dtype), window_strides=(1,), padding='VALID',
                                 dimension_numbers=('NWC', 'WIO', 'NWC'), feature_group_count=x.shape[-1])
    return y + b, xp[:, xp.shape[1] - (w.shape[0] - 1):]


def conformer_conv(h, prefix, w_pw1, w_dw, b_dw, ln_g, ln_b, w_pw2):
    a, gt = jnp.split(h @ w_pw1, 2, axis=-1)
    glu = a * jax.nn.sigmoid(gt)
    y, new_prefix = causal_dwconv(glu, prefix, w_dw, b_dw)
    y = jax.nn.silu(layernorm(y, ln_g, ln_b))
    return y @ w_pw2, new_prefix


def fox_block(q, k, v, fq, fk, q_pos, k_pos):
    s = jnp.einsum('bqhd,bkhd->bhqk', q, k, preferred_element_type=jnp.float32) * (HEAD_DIM ** -0.5)
    bias = jnp.transpose(fq, (0, 2, 1))[:, :, :, None] - jnp.transpose(fk, (0, 2, 1))[:, :, None, :]
    s = jnp.where((k_pos[None, :] <= q_pos[:, None])[None, None], s + bias, NEG_INF)
    p = jax.nn.softmax(s, axis=-1)
    return jnp.einsum('bhqk,bkhd->bqhd', p.astype(v.dtype), v)


def fox_attention(q, k, v, fq, fk, q_pos, k_pos):
    B, Tq, H, dh = q.shape
    if Tq <= Q_BLOCK:
        return fox_block(q, k, v, fq, fk, q_pos, k_pos)
    n_blocks = Tq // Q_BLOCK

    def one_block(i):
        s0 = i * Q_BLOCK
        qb = lax.dynamic_slice_in_dim(q, s0, Q_BLOCK, axis=1)
        fqb = lax.dynamic_slice_in_dim(fq, s0, Q_BLOCK, axis=1)
        pb = lax.dynamic_slice_in_dim(q_pos, s0, Q_BLOCK, axis=0)
        return fox_block(qb, k, v, fqb, fk, pb, k_pos)

    o = lax.map(one_block, jnp.arange(n_blocks))
    return jnp.transpose(o, (1, 0, 2, 3, 4)).reshape(B, Tq, H, dh)


def fox_mixer(h, past_k, past_v, past_logf, w_qkvf, b_f, w_o):
    B, T, D = h.shape
    proj = h @ w_qkvf
    q = proj[..., :D].reshape(B, T, N_HEADS, HEAD_DIM)
    k = proj[..., D:2 * D].reshape(B, T, N_HEADS, HEAD_DIM)
    v = proj[..., 2 * D:3 * D].reshape(B, T, N_HEADS, HEAD_DIM)
    logf = jax.nn.log_sigmoid((proj[..., 3 * D:] + b_f).astype(jnp.float32))
    if past_k is None:
        past_len = 0
        k_all, v_all, logf_all = k, v, logf
    else:
        past_len = past_k.shape[1]
        k_all = jnp.concatenate([past_k, k], axis=1)
        v_all = jnp.concatenate([past_v, v], axis=1)
        logf_all = jnp.concatenate([past_logf.astype(jnp.float32), logf], axis=1)
    F = jnp.cumsum(logf_all, axis=1)
    q_pos = past_len + jnp.arange(T, dtype=jnp.int32)
    k_pos = jnp.arange(past_len + T, dtype=jnp.int32)
    o = fox_attention(q, k_all, v_all, F[:, past_len:], F, q_pos, k_pos)
    return o.reshape(B, T, D) @ w_o, k, v, logf


def conv_ffn(h, prefix, w_up, w_dw, b_dw, w_down):
    u, new_prefix = causal_dwconv(h @ w_up, prefix, w_dw, b_dw)
    a, b = jnp.split(u, 2, axis=-1)
    return (jax.nn.silu(a) * b) @ w_down, new_prefix


def run_trunk(x, c, conv_prefix, ffn_prefix, cache_k, cache_v, cache_logf, page_table,
              w_ada, b_ada, g_mix, g_ffn, g_out, w_pw1, w_dw, b_dw, ln_g, ln_b, w_pw2,
              w_qkvf, b_f, w_o, w_up, w_ffn_dw, b_ffn_dw, w_down):
    new_k, new_v, new_logf, new_conv, new_ffn = [], [], [], [], []
    cs = jax.nn.silu(c)
    for i in range(DEPTH):
        mod = cs @ w_ada[i] + b_ada[i]
        sh_m, sc_m, gt_m, sh_f, sc_f, gt_f = jnp.split(mod, N_MOD, axis=-1)
        h = modulate(rmsnorm(x, g_mix[i]), sh_m, sc_m)
        j = i // N_MIXERS
        if i % N_MIXERS == 0:
            y, buf = conformer_conv(h, conv_prefix[j], w_pw1[j], w_dw[j], b_dw[j], ln_g[j], ln_b[j], w_pw2[j])
            new_conv.append(buf)
        else:
            if page_table is None:
                pk = pv = pf = None
            else:
                nb = page_table.shape[0]
                pk = cache_k[j, page_table].reshape(nb, -1, N_HEADS, HEAD_DIM)
                pv = cache_v[j, page_table].reshape(nb, -1, N_HEADS, HEAD_DIM)
                pf = cache_logf[j, page_table].reshape(nb, -1, N_HEADS)
            y, k, v, lf = fox_mixer(h, pk, pv, pf, w_qkvf[j], b_f[j], w_o[j])
            new_k.append(k)
            new_v.append(v)
            new_logf.append(lf)
        x = x + gt_m[:, None, :] * y
        h = modulate(rmsnorm(x, g_ffn[i]), sh_f, sc_f)
        y, buf = conv_ffn(h, ffn_prefix[i], w_up[i], w_ffn_dw[i], b_ffn_dw[i], w_down[i])
        new_ffn.append(buf)
        x = x + gt_f[:, None, :] * y
    return (rmsnorm(x, g_out), jnp.stack(new_k), jnp.stack(new_v), jnp.stack(new_logf),
            jnp.stack(new_conv), jnp.stack(new_ffn))


def setup_inputs(seed: int = 0) -> dict:
    key = jax.random.key(seed)
    ks = jax.random.split(key, 32)
    n_pages = PAST_LEN // PAGE_SIZE
    n_used = DEC_BATCH * n_pages
    n_pool = n_used + max(1, n_used // 4)
    d, h, f2 = D_MODEL, N_HEADS, 2 * D_FF

    def nrm(k, shape, s=1.0):
        return s * jax.random.normal(k, shape, jnp.float32)

    page_table = jax.random.permutation(ks[0], n_pool)[:n_used].reshape(DEC_BATCH, n_pages).astype(jnp.int32)
    return {
        'x_prompt': nrm(ks[1], (BATCH, SEQ, d)),
        'x_sample': nrm(ks[2], (DEC_BATCH, DEC_SEQ, d)),
        'c_prompt': nrm(ks[3], (BATCH, d)),
        'c_sample': nrm(ks[4], (DEC_BATCH, d)),
        'cache_k': nrm(ks[5], (N_ATTN_LAYERS, n_pool, PAGE_SIZE, h, HEAD_DIM)),
        'cache_v': nrm(ks[6], (N_ATTN_LAYERS, n_pool, PAGE_SIZE, h, HEAD_DIM)),
        'cache_logf': jax.nn.log_sigmoid(3.0 + nrm(ks[7], (N_ATTN_LAYERS, n_pool, PAGE_SIZE, h))),
        'state_conv': nrm(ks[8], (N_CONV_LAYERS, DEC_BATCH, CONV_WIDTH - 1, d), 0.5),
        'state_ffn': nrm(ks[9], (DEPTH, DEC_BATCH, FFN_CONV_WIDTH - 1, f2)),
        'page_table': page_table,
        'w_ada': nrm(ks[10], (DEPTH, d, N_MOD * d), 0.5 * d ** -0.5),
        'b_ada': nrm(ks[11], (DEPTH, N_MOD * d), 0.01),
        'g_mix': 1.0 + nrm(ks[12], (DEPTH, d), 0.01),
        'g_ffn': 1.0 + nrm(ks[13], (DEPTH, d), 0.01),
        'g_out': 1.0 + nrm(ks[14], (d,), 0.01),
        'w_pw1': nrm(ks[15], (N_CONV_LAYERS, d, 2 * d), d ** -0.5),
        'w_dw': nrm(ks[16], (N_CONV_LAYERS, CONV_WIDTH, d), CONV_WIDTH ** -0.5),
        'b_dw': nrm(ks[17], (N_CONV_LAYERS, d), 0.01),
        'ln_g': 1.0 + nrm(ks[18], (N_CONV_LAYERS, d), 0.01),
        'ln_b': nrm(ks[19], (N_CONV_LAYERS, d), 0.01),
        'w_pw2': nrm(ks[20], (N_CONV_LAYERS, d, d), d ** -0.5),
        'w_qkvf': nrm(ks[21], (N_ATTN_LAYERS, d, 3 * d + h), d ** -0.5),
        'b_f': jax.random.uniform(ks[22], (N_ATTN_LAYERS, h), jnp.float32, 2.0, 5.0),
        'w_o': nrm(ks[23], (N_ATTN_LAYERS, d, d), d ** -0.5),
        'w_up': nrm(ks[24], (DEPTH, d, f2), d ** -0.5),
        'w_ffn_dw': nrm(ks[25], (DEPTH, FFN_CONV_WIDTH, f2), FFN_CONV_WIDTH ** -0.5),
        'b_ffn_dw': nrm(ks[26], (DEPTH, f2), 0.01),
        'w_down': nrm(ks[27], (DEPTH, D_FF, d), D_FF ** -0.5),
    }


def reference(x_prompt, x_sample, c_prompt, c_sample, cache_k, cache_v, cache_logf, state_conv, state_ffn,
              page_table, w_ada, b_ada, g_mix, g_ffn, g_out, w_pw1, w_dw, b_dw, ln_g, ln_b, w_pw2,
              w_qkvf, b_f, w_o, w_up, w_ffn_dw, b_ffn_dw, w_down):
    conv0 = jnp.zeros((N_CONV_LAYERS, x_prompt.shape[0], CONV_WIDTH - 1, D_MODEL), x_prompt.dtype)
    ffn0 = jnp.zeros((DEPTH, x_prompt.shape[0], FFN_CONV_WIDTH - 1, 2 * D_FF), x_prompt.dtype)
    y_prompt, k_p, v_p, lf_p, conv_p, ffn_p = run_trunk(
        x_prompt, c_prompt, conv0, ffn0, None, None, None, None,
        w_ada, b_ada, g_mix, g_ffn, g_out, w_pw1, w_dw, b_dw, ln_g, ln_b, w_pw2,
        w_qkvf, b_f, w_o, w_up, w_ffn_dw, b_ffn_dw, w_down)
    y_sample, k_s, v_s, lf_s, conv_s, ffn_s = run_trunk(
        x_sample, c_sample, state_conv, state_ffn, cache_k, cache_v, cache_logf, page_table,
        w_ada, b_ada, g_mix, g_ffn, g_out, w_pw1, w_dw, b_dw, ln_g, ln_b, w_pw2,
        w_qkvf, b_f, w_o, w_up, w_ffn_dw, b_ffn_dw, w_down)
    return (y_prompt, y_sample, k_p, v_p, lf_p, conv_p, ffn_p, k_s, v_s, lf_s, conv_s, ffn_s)
```

```python
import functools

import jax
import jax.numpy as jnp
from jax import lax
from jax.experimental import pallas as pl
from jax.experimental.pallas import tpu as pltpu

F32 = jnp.float32
BF16 = jnp.bfloat16
EPS = 1e-6
NEG_INF = -1e30

V7X_VMEM_BYTES = 64 * 1024 * 1024
VMEM_LIMIT = V7X_VMEM_BYTES - 8 * 1024 * 1024
LANES = 128
SUBLANES = 8

MM_ROWS = 1024
MM_ROWS_BIG_K = 512
MM_COLS = 512
MM_COLS_BIG_K = 256
MM_SMALL_K = 4096
NORM_ROWS = 256
CONV_BLOCK_ROWS = 256
LOGF_ROWS = 512
ATTN_ROWS = 512
FFN_SMALL_COLS = 5504
ADA_COLS = 512


def _pick(dim, pref, align):
    if dim <= pref:
        return dim
    t = (pref // align) * align
    while t >= align:
        if dim % t == 0:
            return t
        t -= align
    return dim


def _params(n_axes):
    return pltpu.CompilerParams(
        dimension_semantics=("arbitrary",) * n_axes,
        vmem_limit_bytes=VMEM_LIMIT)


def _silu(x):
    return x * jax.nn.sigmoid(x)


def _cumsum_rows(x):
    rows = x.shape[0]
    idx = lax.broadcasted_iota(jnp.int32, x.shape, 0)
    s = 1
    while s < rows:
        x = x + jnp.where(idx >= s, pltpu.roll(x, s, 0), 0.0)
        s *= 2
    return x


def _ada_kernel(c_ref, w_ref, b_ref, o_ref):
    cs = _silu(c_ref[...]).astype(BF16)
    o_ref[...] = jnp.dot(cs, w_ref[...].astype(BF16),
                         preferred_element_type=F32) + b_ref[...]


def _ada(c, w_ada, b_ada):
    depth, d, n = w_ada.shape
    rows = c.shape[0]
    tn = _pick(n, ADA_COLS, LANES)
    return pl.pallas_call(
        _ada_kernel,
        grid=(depth, n // tn),
        in_specs=[pl.BlockSpec((rows, d), lambda i, j: (0, 0)),
                  pl.BlockSpec((None, d, tn), lambda i, j: (i, 0, j)),
                  pl.BlockSpec((None, 1, tn), lambda i, j: (i, 0, j))],
        out_specs=pl.BlockSpec((None, rows, tn), lambda i, j: (i, 0, j)),
        out_shape=jax.ShapeDtypeStruct((depth, rows, n), F32),
        compiler_params=_params(2),
        name="ada",
    )(c, w_ada, b_ada.reshape(depth, 1, n))


def _norm_mod_kernel(x_ref, g_ref, sh_ref, sc_ref, o_ref):
    x = x_ref[...]
    y = x * lax.rsqrt(jnp.mean(x * x, axis=-1, keepdims=True) + EPS) * g_ref[...]
    o_ref[...] = (y * (1.0 + sc_ref[...]) + sh_ref[...]).astype(o_ref.dtype)


def _norm_kernel(x_ref, g_ref, o_ref):
    x = x_ref[...]
    o_ref[...] = x * lax.rsqrt(jnp.mean(x * x, axis=-1, keepdims=True) + EPS) * g_ref[...]


def _norm_mod(x, g, shift, scale):
    b, t, d = x.shape
    tt = _pick(t, NORM_ROWS, 16)
    return pl.pallas_call(
        _norm_mod_kernel,
        grid=(b, t // tt),
        in_specs=[pl.BlockSpec((None, tt, d), lambda i, j: (i, j, 0)),
                  pl.BlockSpec((1, d), lambda i, j: (0, 0)),
                  pl.BlockSpec((None, 1, d), lambda i, j: (i, 0, 0)),
                  pl.BlockSpec((None, 1, d), lambda i, j: (i, 0, 0))],
        out_specs=pl.BlockSpec((None, tt, d), lambda i, j: (i, j, 0)),
        out_shape=jax.ShapeDtypeStruct((b, t, d), BF16),
        compiler_params=_params(2),
        name="norm_mod",
    )(x, g.reshape(1, d), shift.reshape(b, 1, d), scale.reshape(b, 1, d))


def _final_norm(x, g):
    b, t, d = x.shape
    tt = _pick(t, NORM_ROWS, SUBLANES)
    return pl.pallas_call(
        _norm_kernel,
        grid=(b, t // tt),
        in_specs=[pl.BlockSpec((None, tt, d), lambda i, j: (i, j, 0)),
                  pl.BlockSpec((1, d), lambda i, j: (0, 0))],
        out_specs=pl.BlockSpec((None, tt, d), lambda i, j: (i, j, 0)),
        out_shape=jax.ShapeDtypeStruct((b, t, d), F32),
        compiler_params=_params(2),
        name="final_norm",
    )(x, g.reshape(1, d))


def _cast_once(w_ref, wb_ref):
    @pl.when(pl.program_id(1) == 0)
    def _():
        wb_ref[...] = w_ref[...].astype(BF16)


def _mm_plain_kernel(a_ref, w_ref, o_ref, wb_ref):
    _cast_once(w_ref, wb_ref)
    o_ref[...] = jnp.dot(a_ref[...], wb_ref[...], preferred_element_type=F32)


def _mm_glu_kernel(a_ref, wa_ref, wg_ref, o_ref, wab_ref, wgb_ref):
    _cast_once(wa_ref, wab_ref)
    _cast_once(wg_ref, wgb_ref)
    a = a_ref[...]
    va = jnp.dot(a, wab_ref[...], preferred_element_type=F32)
    vg = jnp.dot(a, wgb_ref[...], preferred_element_type=F32)
    o_ref[...] = va * jax.nn.sigmoid(vg)


def _mm_res_kernel(a_ref, w_ref, x_ref, g_ref, o_ref, wb_ref):
    _cast_once(w_ref, wb_ref)
    acc = jnp.dot(a_ref[...], wb_ref[...], preferred_element_type=F32)
    o_ref[...] = x_ref[...] + g_ref[...] * acc


def _mm_tiles(m, k):
    tm = _pick(m, MM_ROWS if k <= MM_SMALL_K else MM_ROWS_BIG_K, 16)
    tn_pref = MM_COLS if k <= MM_SMALL_K else MM_COLS_BIG_K
    return tm, tn_pref


def _mm_plain(a, w, layer, col0, ncols):
    m, k = a.shape
    tm, tn_pref = _mm_tiles(m, k)
    tn = _pick(ncols, tn_pref, LANES)
    nb0 = col0 // tn
    assert col0 % tn == 0
    return pl.pallas_call(
        _mm_plain_kernel,
        grid=(ncols // tn, m // tm),
        in_specs=[pl.BlockSpec((tm, k), lambda n, i: (i, 0)),
                  pl.BlockSpec((None, k, tn), lambda n, i: (layer, 0, n + nb0))],
        out_specs=pl.BlockSpec((tm, tn), lambda n, i: (i, n)),
        out_shape=jax.ShapeDtypeStruct((m, ncols), F32),
        scratch_shapes=[pltpu.VMEM((k, tn), BF16)],
        compiler_params=_params(2),
        name="mm_plain",
    )(a, w)


def _mm_glu(a, w, layer):
    m, k = a.shape
    d = w.shape[2] // 2
    tm, tn_pref = _mm_tiles(m, k)
    tn = _pick(d, tn_pref // 2, LANES)
    nbg = d // tn
    return pl.pallas_call(
        _mm_glu_kernel,
        grid=(d // tn, m // tm),
        in_specs=[pl.BlockSpec((tm, k), lambda n, i: (i, 0)),
                  pl.BlockSpec((None, k, tn), lambda n, i: (layer, 0, n)),
                  pl.BlockSpec((None, k, tn), lambda n, i: (layer, 0, n + nbg))],
        out_specs=pl.BlockSpec((tm, tn), lambda n, i: (i, n)),
        out_shape=jax.ShapeDtypeStruct((m, d), F32),
        scratch_shapes=[pltpu.VMEM((k, tn), BF16), pltpu.VMEM((k, tn), BF16)],
        compiler_params=_params(2),
        name="mm_glu",
    )(a, w, w)


def _mm_res(a, w, layer, x, gate, rows_per_gate):
    m, k = a.shape
    n_out = w.shape[2]
    tm, tn_pref = _mm_tiles(m, k)
    tn = _pick(n_out, tn_pref, LANES)
    r = gate.shape[1]
    assert rows_per_gate % tm == 0 and r in (1, tm)
    return pl.pallas_call(
        _mm_res_kernel,
        grid=(n_out // tn, m // tm),
        in_specs=[pl.BlockSpec((tm, k), lambda n, i: (i, 0)),
                  pl.BlockSpec((None, k, tn), lambda n, i: (layer, 0, n)),
                  pl.BlockSpec((tm, tn), lambda n, i: (i, n)),
                  pl.BlockSpec((None, r, tn),
                               lambda n, i: ((i * tm) // rows_per_gate, 0, n))],
        out_specs=pl.BlockSpec((tm, tn), lambda n, i: (i, n)),
        out_shape=jax.ShapeDtypeStruct((m, n_out), F32),
        scratch_shapes=[pltpu.VMEM((k, tn), BF16)],
        compiler_params=_params(2),
        name="mm_res",
    )(a, w, x, gate)


def _mm_ffn_kernel(a_ref, wa_ref, wb_ref, cwa_ref, cwb_ref, cba_ref, cbb_ref,
                   o_ref, sta_ref, stb_ref, wab_ref, wbb_ref, ua_ref, ub_ref,
                   *, tm, taps, tiles_per_seq):
    _cast_once(wa_ref, wab_ref)
    _cast_once(wb_ref, wbb_ref)

    @pl.when(pl.program_id(1) % tiles_per_seq == 0)
    def _():
        ua_ref[0:SUBLANES, :] = jnp.zeros((SUBLANES, ua_ref.shape[1]), F32)
        ub_ref[0:SUBLANES, :] = jnp.zeros((SUBLANES, ub_ref.shape[1]), F32)

    a = a_ref[...]
    ua_ref[SUBLANES:SUBLANES + tm, :] = jnp.dot(a, wab_ref[...], preferred_element_type=F32)
    ub_ref[SUBLANES:SUBLANES + tm, :] = jnp.dot(a, wbb_ref[...], preferred_element_type=F32)

    def conv(u_ref, cw_ref, cb_ref):
        y = cb_ref[...]
        for j in range(taps):
            r0 = SUBLANES - (taps - 1) + j
            y = y + cw_ref[j:j + 1, :] * u_ref[r0:r0 + tm, :]
        return y

    ya = conv(ua_ref, cwa_ref, cba_ref)
    yb = conv(ub_ref, cwb_ref, cbb_ref)
    o_ref[...] = (_silu(ya) * yb).astype(o_ref.dtype)

    keep = taps - 1
    sta_ref[...] = ua_ref[SUBLANES + tm - keep:SUBLANES + tm, :]
    stb_ref[...] = ub_ref[SUBLANES + tm - keep:SUBLANES + tm, :]
    ua_ref[0:SUBLANES, :] = ua_ref[tm:tm + SUBLANES, :]
    ub_ref[0:SUBLANES, :] = ub_ref[tm:tm + SUBLANES, :]


def _mm_ffn(a, w_up, cw, cb, layer, seq_len):
    m, k = a.shape
    f = w_up.shape[2] // 2
    taps = cw.shape[1]
    assert taps - 1 <= SUBLANES
    tm = _pick(seq_len, MM_ROWS, 16)
    tn = _pick(f, MM_COLS // 2, LANES)
    nbh = f // tn
    tiles_per_seq = seq_len // tm
    nseq = m // seq_len
    depth = cb.shape[0]
    cb3 = cb.reshape(depth, 1, 2 * f)
    kern = functools.partial(_mm_ffn_kernel, tm=tm, taps=taps, tiles_per_seq=tiles_per_seq)
    return pl.pallas_call(
        kern,
        grid=(f // tn, m // tm),
        in_specs=[pl.BlockSpec((tm, k), lambda n, i: (i, 0)),
                  pl.BlockSpec((None, k, tn), lambda n, i: (layer, 0, n)),
                  pl.BlockSpec((None, k, tn), lambda n, i: (layer, 0, n + nbh)),
                  pl.BlockSpec((None, taps, tn), lambda n, i: (layer, 0, n)),
                  pl.BlockSpec((None, taps, tn), lambda n, i: (layer, 0, n + nbh)),
                  pl.BlockSpec((None, 1, tn), lambda n, i: (layer, 0, n)),
                  pl.BlockSpec((None, 1, tn), lambda n, i: (layer, 0, n + nbh))],
        out_specs=[pl.BlockSpec((tm, tn), lambda n, i: (i, n)),
                   pl.BlockSpec((None, taps - 1, tn), lambda n, i: (i // tiles_per_seq, 0, n)),
                   pl.BlockSpec((None, taps - 1, tn), lambda n, i: (i // tiles_per_seq, 0, n))],
        out_shape=[jax.ShapeDtypeStruct((m, f), BF16),
                   jax.ShapeDtypeStruct((nseq, taps - 1, f), F32),
                   jax.ShapeDtypeStruct((nseq, taps - 1, f), F32)],
        scratch_shapes=[pltpu.VMEM((k, tn), BF16), pltpu.VMEM((k, tn), BF16),
                        pltpu.VMEM((SUBLANES + tm, tn), F32),
                        pltpu.VMEM((SUBLANES + tm, tn), F32)],
        compiler_params=_params(2),
        name="mm_ffn",
    )(a, w_up, w_up, cw, cw, cb3, cb3)


def _ffn_small_kernel(xa_ref, xb_ref, cwa_ref, cwb_ref, cba_ref, cbb_ref, o_ref, *, t, taps):
    def conv(x_ref, cw_ref, cb_ref):
        y = cb_ref[...][None]
        for j in range(taps):
            y = y + cw_ref[j:j + 1, :][None] * x_ref[:, j:j + t, :]
        return y

    ya = conv(xa_ref, cwa_ref, cba_ref)
    yb = conv(xb_ref, cwb_ref, cbb_ref)
    o_ref[...] = (_silu(ya) * yb).astype(o_ref.dtype)


def _ffn_small(xp, cw, cb, layer):
    b, rows, f2 = xp.shape
    f = f2 // 2
    taps = cw.shape[1]
    t = rows - (taps - 1)
    tn = _pick(f, FFN_SMALL_COLS, LANES)
    nbh = f // tn
    depth = cb.shape[0]
    cb3 = cb.reshape(depth, 1, f2)
    kern = functools.partial(_ffn_small_kernel, t=t, taps=taps)
    return pl.pallas_call(
        kern,
        grid=(f // tn,),
        in_specs=[pl.BlockSpec((b, rows, tn), lambda n: (0, 0, n)),
                  pl.BlockSpec((b, rows, tn), lambda n: (0, 0, n + nbh)),
                  pl.BlockSpec((None, taps, tn), lambda n: (layer, 0, n)),
                  pl.BlockSpec((None, taps, tn), lambda n: (layer, 0, n + nbh)),
                  pl.BlockSpec((None, 1, tn), lambda n: (layer, 0, n)),
                  pl.BlockSpec((None, 1, tn), lambda n: (layer, 0, n + nbh))],
        out_specs=pl.BlockSpec((b, t, tn), lambda n: (0, 0, n)),
        out_shape=jax.ShapeDtypeStruct((b, t, f), BF16),
        compiler_params=_params(1),
        name="ffn_small",
    )(xp, xp, cw, cw, cb3, cb3)


CONV_ROWS = 16
CONV_LANES = 512


def _ln_silu(y, g, b):
    mu = jnp.mean(y, axis=-1, keepdims=True)
    yc = y - mu
    var = jnp.mean(yc * yc, axis=-1, keepdims=True)
    return _silu(yc * lax.rsqrt(var + EPS) * g + b)


def _conv_ln_kernel(x_ref, w_ref, b_ref, g_ref, be_ref, o_ref, buf_ref, y_ref,
                    *, tt, taps, halo):
    d = x_ref.shape[1]

    @pl.when(pl.program_id(1) == 0)
    def _():
        buf_ref[0:halo, :] = jnp.zeros((halo, d), F32)

    buf_ref[halo:halo + tt, :] = x_ref[...]
    base = halo - (taps - 1)
    lane_chunk = min(CONV_LANES, d)
    win_rows = halo + CONV_ROWS

    def rows_step(r, carry):
        r0 = pl.multiple_of(r * CONV_ROWS, CONV_ROWS)
        for c0 in range(0, d, lane_chunk):
            win = buf_ref[pl.ds(r0, win_rows), c0:c0 + lane_chunk]
            acc = jnp.broadcast_to(b_ref[:, c0:c0 + lane_chunk], (CONV_ROWS, lane_chunk))
            for j in range(taps):
                acc = acc + (w_ref[j:j + 1, c0:c0 + lane_chunk]
                             * win[base + j:base + j + CONV_ROWS, :])
            y_ref[:, c0:c0 + lane_chunk] = acc
        o_ref[pl.ds(r0, CONV_ROWS), :] = _ln_silu(
            y_ref[...], g_ref[...], be_ref[...]).astype(o_ref.dtype)
        return carry

    lax.fori_loop(0, tt // CONV_ROWS, rows_step, 0)
    buf_ref[0:halo, :] = buf_ref[tt:tt + halo, :]


def _conv_ln(x, w_dw, b_dw, ln_g, ln_b, layer):
    b, t, d = x.shape
    nl, taps, _ = w_dw.shape
    halo = -(-(taps - 1) // SUBLANES) * SUBLANES
    tt = _pick(t, CONV_BLOCK_ROWS, CONV_ROWS)
    assert tt % CONV_ROWS == 0 and tt >= halo
    kern = functools.partial(_conv_ln_kernel, tt=tt, taps=taps, halo=halo)
    vec = lambda a: a.reshape(nl, 1, d)
    vspec = pl.BlockSpec((None, 1, d), lambda i, j: (layer, 0, 0))
    return pl.pallas_call(
        kern,
        grid=(b, t // tt),
        in_specs=[pl.BlockSpec((None, tt, d), lambda i, j: (i, j, 0)),
                  pl.BlockSpec((None, taps, d), lambda i, j: (layer, 0, 0)),
                  vspec, vspec, vspec],
        out_specs=pl.BlockSpec((None, tt, d), lambda i, j: (i, j, 0)),
        out_shape=jax.ShapeDtypeStruct((b, t, d), BF16),
        scratch_shapes=[pltpu.VMEM((halo + tt, d), F32),
                        pltpu.VMEM((CONV_ROWS, d), F32)],
        compiler_params=_params(2),
        name="conv_ln",
    )(x, w_dw, vec(b_dw), vec(ln_g), vec(ln_b))


def _conv_small_kernel(xp_ref, w_ref, b_ref, g_ref, be_ref, o_ref, *, t, taps):
    acc = jnp.broadcast_to(b_ref[...], (t, xp_ref.shape[1]))
    for j in range(taps):
        acc = acc + w_ref[j:j + 1, :] * xp_ref[j:j + t, :]
    o_ref[...] = _ln_silu(acc, g_ref[...], be_ref[...]).astype(o_ref.dtype)


def _conv_small(xp, w_dw, b_dw, ln_g, ln_b, layer):
    b, rows, d = xp.shape
    nl, taps, _ = w_dw.shape
    t = rows - (taps - 1)
    kern = functools.partial(_conv_small_kernel, t=t, taps=taps)
    vec = lambda a: a.reshape(nl, 1, d)
    vspec = pl.BlockSpec((None, 1, d), lambda i: (layer, 0, 0))
    return pl.pallas_call(
        kern,
        grid=(b,),
        in_specs=[pl.BlockSpec((None, rows, d), lambda i: (i, 0, 0)),
                  pl.BlockSpec((None, taps, d), lambda i: (layer, 0, 0)),
                  vspec, vspec, vspec],
        out_specs=pl.BlockSpec((None, t, d), lambda i: (i, 0, 0)),
        out_shape=jax.ShapeDtypeStruct((b, t, d), BF16),
        compiler_params=_params(1),
        name="conv_small",
    )(xp, w_dw, vec(b_dw), vec(ln_g), vec(ln_b))


def _log_sigmoid(z):
    return jnp.minimum(z, 0.0) - jnp.log1p(jnp.exp(-jnp.abs(z)))


def _logf_kernel(a_ref, w_ref, b_ref, lf_ref, nf_ref, carry_ref, *, tiles_per_seq):
    @pl.when(pl.program_id(0) % tiles_per_seq == 0)
    def _():
        carry_ref[...] = jnp.zeros(carry_ref.shape, F32)

    z = jnp.dot(a_ref[...], w_ref[...].astype(BF16), preferred_element_type=F32) + b_ref[...]
    lf = _log_sigmoid(z)
    lf_ref[...] = lf
    c = _cumsum_rows(lf) + carry_ref[...]
    nf_ref[...] = -c
    tm = lf.shape[0]
    carry_ref[...] = c[tm - 1:tm, :]


def _logf(a, w_f, b_f, seq_len):
    m, k = a.shape
    tm = _pick(seq_len, LOGF_ROWS, 16)
    kern = functools.partial(_logf_kernel, tiles_per_seq=seq_len // tm)
    return pl.pallas_call(
        kern,
        grid=(m // tm,),
        in_specs=[pl.BlockSpec((tm, k), lambda i: (i, 0)),
                  pl.BlockSpec((k, LANES), lambda i: (0, 0)),
                  pl.BlockSpec((1, LANES), lambda i: (0, 0))],
        out_specs=[pl.BlockSpec((tm, LANES), lambda i: (i, 0)),
                   pl.BlockSpec((tm, LANES), lambda i: (i, 0))],
        out_shape=[jax.ShapeDtypeStruct((m, LANES), F32),
                   jax.ShapeDtypeStruct((m, LANES), F32)],
        scratch_shapes=[pltpu.VMEM((1, LANES), F32)],
        compiler_params=_params(1),
        name="logf",
    )(a, w_f, b_f)


def _flash_step(s, v, m_ref, l_ref, acc_ref):
    m_prev = m_ref[...]
    m_new = jnp.maximum(m_prev, jnp.max(s, axis=1, keepdims=True))
    alpha = jnp.exp(m_prev - m_new)
    p = jnp.exp(s - m_new)
    l_ref[...] = alpha * l_ref[...] + jnp.sum(p, axis=1, keepdims=True)
    acc_ref[...] = alpha * acc_ref[...] + jnp.dot(
        p.astype(BF16), v, preferred_element_type=F32)
    m_ref[...] = m_new


def _attn_kernel(q_ref, k_ref, v_ref, nf_ref, o_ref, m_ref, l_ref, acc_ref, *, tq, scale):
    qi = pl.program_id(2)
    ki = pl.program_id(3)

    @pl.when(ki == 0)
    def _():
        m_ref[...] = jnp.full(m_ref.shape, NEG_INF, F32)
        l_ref[...] = jnp.zeros(l_ref.shape, F32)
        acc_ref[...] = jnp.zeros(acc_ref.shape, F32)

    @pl.when(ki <= qi)
    def _():
        s = lax.dot_general(q_ref[...].astype(BF16), k_ref[...].astype(BF16),
                            (((1,), (1,)), ((), ())), preferred_element_type=F32)
        s = s * scale + nf_ref[...]
        row = qi * tq + lax.broadcasted_iota(jnp.int32, s.shape, 0)
        col = ki * tq + lax.broadcasted_iota(jnp.int32, s.shape, 1)
        s = jnp.where(col <= row, s, NEG_INF)
        _flash_step(s, v_ref[...].astype(BF16), m_ref, l_ref, acc_ref)

    @pl.when(ki == qi)
    def _():
        o_ref[...] = (acc_ref[...] / l_ref[...]).astype(o_ref.dtype)


def _attention(q, k, v, negf_rows, n_heads):
    b, t, d = q.shape
    dh = d // n_heads
    tq = _pick(t, ATTN_ROWS, LANES)
    nq = t // tq
    kern = functools.partial(_attn_kernel, tq=tq, scale=dh ** -0.5)
    qspec = pl.BlockSpec((None, tq, dh), lambda bi, h, qi, ki: (bi, qi, h))
    kspec = pl.BlockSpec((None, tq, dh), lambda bi, h, qi, ki: (bi, jnp.minimum(ki, qi), h))
    return pl.pallas_call(
        kern,
        grid=(b, n_heads, nq, nq),
        in_specs=[qspec, kspec, kspec,
                  pl.BlockSpec((None, None, 1, tq),
                               lambda bi, h, qi, ki: (bi, h, 0, jnp.minimum(ki, qi)))],
        out_specs=qspec,
        out_shape=jax.ShapeDtypeStruct((b, t, d), BF16),
        scratch_shapes=[pltpu.VMEM((tq, 1), F32), pltpu.VMEM((tq, 1), F32),
                        pltpu.VMEM((tq, dh), F32)],
        compiler_params=_params(4),
        name="attention",
    )(q, k, v, negf_rows)


def _page_cumsum_kernel(pt_ref, past_ref, new_ref, o_ref, carry_ref, *, n_pages):
    p = pl.program_id(1)

    @pl.when(p == 0)
    def _():
        carry_ref[...] = jnp.zeros(carry_ref.shape, F32)

    def run(src_ref):
        c = _cumsum_rows(src_ref[...]) + carry_ref[...]
        o_ref[...] = -c
        rows = c.shape[0]
        carry_ref[...] = c[rows - 1:rows, :]

    @pl.when(p < n_pages)
    def _():
        run(past_ref)

    @pl.when(p == n_pages)
    def _():
        run(new_ref)


def _page_cumsum(page_table, cache_logf, logf_new_pad, layer):
    nb, n_pages = page_table.shape
    _, _, page, h = cache_logf.shape
    kern = functools.partial(_page_cumsum_kernel, n_pages=n_pages)
    grid_spec = pltpu.PrefetchScalarGridSpec(
        num_scalar_prefetch=1,
        grid=(nb, n_pages + 1),
        in_specs=[pl.BlockSpec((None, None, page, h),
                               lambda bi, p, pt: (layer, pt[bi * n_pages + jnp.minimum(p, n_pages - 1)], 0, 0)),
                  pl.BlockSpec((None, page, h), lambda bi, p, pt: (bi, 0, 0))],
        out_specs=pl.BlockSpec((None, None, page, h), lambda bi, p, pt: (bi, p, 0, 0)),
        scratch_shapes=[pltpu.VMEM((1, h), F32)])
    return pl.pallas_call(
        kern,
        grid_spec=grid_spec,
        out_shape=jax.ShapeDtypeStruct((nb, n_pages + 1, page, h), F32),
        compiler_params=_params(2),
        name="page_cumsum",
    )(page_table.reshape(-1), cache_logf, logf_new_pad)


def _decode_kernel(pt_ref, q_ref, kp_ref, vp_ref, kn_ref, vn_ref, nf_ref, o_ref,
                   m_ref, l_ref, acc_ref, *, n_pages, n_heads, scale):
    p = pl.program_id(1)

    @pl.when(p == 0)
    def _():
        m_ref[...] = jnp.full(m_ref.shape, NEG_INF, F32)
        l_ref[...] = jnp.zeros(l_ref.shape, F32)
        acc_ref[...] = jnp.zeros(acc_ref.shape, F32)

    def run(k_ref, v_ref, causal):
        s = lax.dot_general(q_ref[...].astype(BF16), k_ref[...].astype(BF16),
                            (((1,), (1,)), ((), ())), preferred_element_type=F32)
        s = s * scale + nf_ref[...]
        row = lax.broadcasted_iota(jnp.int32, s.shape, 0)
        col = lax.broadcasted_iota(jnp.int32, s.shape, 1)
        ok = (row % n_heads) == (col % n_heads)
        if causal:
            ok = ok & ((col // n_heads) <= (row // n_heads))
        s = jnp.where(ok, s, NEG_INF)
        _flash_step(s, v_ref[...].astype(BF16), m_ref, l_ref, acc_ref)

    @pl.when(p < n_pages)
    def _():
        run(kp_ref, vp_ref, False)

    @pl.when(p == n_pages)
    def _():
        run(kn_ref, vn_ref, True)
        o_ref[...] = acc_ref[...] / l_ref[...]


def _decode_attention(page_table, q_rows, cache_k, cache_v, k_new, v_new, negf, layer, n_heads):
    nb, n_pages = page_table.shape
    _, qr, dh = q_rows.shape
    pr = cache_k.shape[2]
    kern = functools.partial(_decode_kernel, n_pages=n_pages, n_heads=n_heads, scale=dh ** -0.5)
    page_spec = pl.BlockSpec(
        (None, None, pr, dh),
        lambda bi, p, pt: (layer, pt[bi * n_pages + jnp.minimum(p, n_pages - 1)], 0, 0))
    new_spec = pl.BlockSpec((None, pr, dh), lambda bi, p, pt: (bi, 0, 0))
    grid_spec = pltpu.PrefetchScalarGridSpec(
        num_scalar_prefetch=1,
        grid=(nb, n_pages + 1),
        in_specs=[pl.BlockSpec((None, qr, dh), lambda bi, p, pt: (bi, 0, 0)),
                  page_spec, page_spec, new_spec, new_spec,
                  pl.BlockSpec((None, None, 1, pr), lambda bi, p, pt: (bi, p, 0, 0))],
        out_specs=pl.BlockSpec((None, qr, dh), lambda bi, p, pt: (bi, 0, 0)),
        scratch_shapes=[pltpu.VMEM((qr, 1), F32), pltpu.VMEM((qr, 1), F32),
                        pltpu.VMEM((qr, dh), F32)])
    return pl.pallas_call(
        kern,
        grid_spec=grid_spec,
        out_shape=jax.ShapeDtypeStruct((nb, qr, dh), F32),
        compiler_params=_params(2),
        name="decode_attention",
    )(page_table.reshape(-1), q_rows, cache_k, cache_v, k_new, v_new, negf)


def _forget_weights(w_qkvf, b_f, layer, d, n_heads):
    w_f = jnp.pad(w_qkvf[layer, :, 3 * d:], ((0, 0), (0, LANES - n_heads)))
    bias = jnp.pad(b_f[layer], (0, LANES - n_heads)).reshape(1, LANES)
    return w_f, bias


def _prompt_trunk(x, mod, p):
    b, t, d = x.shape
    m = b * t
    depth = p['w_up'].shape[0]
    n_heads = p['b_f'].shape[1]
    x2 = x.reshape(m, d)
    new_k, new_v, new_logf, new_conv, new_ffn = [], [], [], [], []
    for i in range(depth):
        sh_m, sc_m, gt_m, sh_f, sc_f, gt_f = jnp.split(mod[i], 6, axis=-1)
        h = _norm_mod(x2.reshape(b, t, d), p['g_mix'][i], sh_m, sc_m).reshape(m, d)
        j = i // 2
        if i % 2 == 0:
            glu = _mm_glu(h, p['w_pw1'], j).reshape(b, t, d)
            taps = p['w_dw'].shape[1]
            new_conv.append(glu[:, t - (taps - 1):])
            y = _conv_ln(glu, p['w_dw'], p['b_dw'], p['ln_g'], p['ln_b'], j).reshape(m, d)
            x2 = _mm_res(y, p['w_pw2'], j, x2, gt_m.reshape(b, 1, d), t)
        else:
            q = _mm_plain(h, p['w_qkvf'], j, 0, d)
            k = _mm_plain(h, p['w_qkvf'], j, d, d)
            v = _mm_plain(h, p['w_qkvf'], j, 2 * d, d)
            w_f, bias_f = _forget_weights(p['w_qkvf'], p['b_f'], j, d, n_heads)
            logf, negf = _logf(h, w_f, bias_f, t)
            negf_rows = jnp.transpose(negf.reshape(b, t, LANES)[:, :, :n_heads], (0, 2, 1))
            o = _attention(q.reshape(b, t, d), k.reshape(b, t, d), v.reshape(b, t, d),
                           negf_rows.reshape(b, n_heads, 1, t), n_heads).reshape(m, d)
            x2 = _mm_res(o, p['w_o'], j, x2, gt_m.reshape(b, 1, d), t)
            new_k.append(k.reshape(b, t, n_heads, d // n_heads))
            new_v.append(v.reshape(b, t, n_heads, d // n_heads))
            new_logf.append(logf.reshape(b, t, LANES)[:, :, :n_heads])
        h = _norm_mod(x2.reshape(b, t, d), p['g_ffn'][i], sh_f, sc_f).reshape(m, d)
        act, st_a, st_b = _mm_ffn(h, p['w_up'], p['w_ffn_dw'], p['b_ffn_dw'], i, t)
        new_ffn.append(jnp.concatenate([st_a, st_b], axis=-1))
        x2 = _mm_res(act, p['w_down'], i, x2, gt_f.reshape(b, 1, d), t)
    y = _final_norm(x2.reshape(b, t, d), p['g_out'])
    return (y, jnp.stack(new_k), jnp.stack(new_v), jnp.stack(new_logf),
            jnp.stack(new_conv), jnp.stack(new_ffn))


def _sample_trunk(x, mod, p, cache_k, cache_v, cache_logf, state_conv, state_ffn, page_table):
    b, t, d = x.shape
    m = b * t
    depth = p['w_up'].shape[0]
    n_heads = p['b_f'].shape[1]
    dh = d // n_heads
    nl, n_pool, page, _, _ = cache_k.shape
    ck = cache_k.reshape(nl, n_pool, page * n_heads, dh)
    cv = cache_v.reshape(nl, n_pool, page * n_heads, dh)
    x2 = x.reshape(m, d)
    rep = lambda g: jnp.repeat(g, t, axis=0).reshape(1, m, d)
    pad_page = lambda a: jnp.pad(a, ((0, 0), (0, page - t)) + ((0, 0),) * (a.ndim - 2))
    new_k, new_v, new_logf, new_conv, new_ffn = [], [], [], [], []
    for i in range(depth):
        sh_m, sc_m, gt_m, sh_f, sc_f, gt_f = jnp.split(mod[i], 6, axis=-1)
        h = _norm_mod(x2.reshape(b, t, d), p['g_mix'][i], sh_m, sc_m).reshape(m, d)
        j = i // 2
        if i % 2 == 0:
            glu = _mm_glu(h, p['w_pw1'], j).reshape(b, t, d)
            xp = jnp.concatenate([state_conv[j], glu], axis=1)
            taps = p['w_dw'].shape[1]
            new_conv.append(xp[:, xp.shape[1] - (taps - 1):])
            y = _conv_small(xp, p['w_dw'], p['b_dw'], p['ln_g'], p['ln_b'], j).reshape(m, d)
            x2 = _mm_res(y, p['w_pw2'], j, x2, rep(gt_m), m)
        else:
            q = _mm_plain(h, p['w_qkvf'], j, 0, d)
            k = _mm_plain(h, p['w_qkvf'], j, d, d)
            v = _mm_plain(h, p['w_qkvf'], j, 2 * d, d)
            w_f, bias_f = _forget_weights(p['w_qkvf'], p['b_f'], j, d, n_heads)
            logf = _logf(h, w_f, bias_f, m)[0].reshape(b, t, LANES)[:, :, :n_heads]
            negf = _page_cumsum(page_table, cache_logf, pad_page(logf), j)
            k_new = pad_page(k.reshape(b, t, n_heads, dh)).reshape(b, page * n_heads, dh)
            v_new = pad_page(v.reshape(b, t, n_heads, dh)).reshape(b, page * n_heads, dh)
            o = _decode_attention(
                page_table, q.reshape(b, t * n_heads, dh), ck, cv, k_new, v_new,
                negf.reshape(b, negf.shape[1], 1, page * n_heads), j, n_heads)
            x2 = _mm_res(o.reshape(m, d).astype(BF16), p['w_o'], j, x2, rep(gt_m), m)
            new_k.append(k.reshape(b, t, n_heads, dh))
            new_v.append(v.reshape(b, t, n_heads, dh))
            new_logf.append(logf)
        h = _norm_mod(x2.reshape(b, t, d), p['g_ffn'][i], sh_f, sc_f).reshape(m, d)
        f2 = p['w_up'].shape[2]
        u = _mm_plain(h, p['w_up'], i, 0, f2).reshape(b, t, f2)
        xp = jnp.concatenate([state_ffn[i], u], axis=1)
        ftaps = p['w_ffn_dw'].shape[1]
        new_ffn.append(xp[:, xp.shape[1] - (ftaps - 1):])
        act = _ffn_small(xp, p['w_ffn_dw'], p['b_ffn_dw'], i).reshape(m, f2 // 2)
        x2 = _mm_res(act, p['w_down'], i, x2, rep(gt_f), m)
    y = _final_norm(x2.reshape(b, t, d), p['g_out'])
    return (y, jnp.stack(new_k), jnp.stack(new_v), jnp.stack(new_logf),
            jnp.stack(new_conv), jnp.stack(new_ffn))


def kernel(x_prompt, x_sample, c_prompt, c_sample, cache_k, cache_v, cache_logf, state_conv, state_ffn, page_table, w_ada, b_ada, g_mix, g_ffn, g_out, w_pw1, w_dw, b_dw, ln_g, ln_b, w_pw2, w_qkvf, b_f, w_o, w_up, w_ffn_dw, b_ffn_dw, w_down):
    p = dict(g_mix=g_mix, g_ffn=g_ffn, g_out=g_out, w_pw1=w_pw1, w_dw=w_dw, b_dw=b_dw,
             ln_g=ln_g, ln_b=ln_b, w_pw2=w_pw2, w_qkvf=w_qkvf, b_f=b_f, w_o=w_o,
             w_up=w_up, w_ffn_dw=w_ffn_dw, b_ffn_dw=b_ffn_dw, w_down=w_down)
    nb_p = c_prompt.shape[0]
    nb_s = c_sample.shape[0]
    c_all = jnp.concatenate([c_prompt, c_sample], axis=0)
    pad = (-c_all.shape[0]) % 16
    c_all = jnp.pad(c_all, ((0, pad), (0, 0)))
    mod = _ada(c_all, w_ada, b_ada)
    outs_p = _prompt_trunk(x_prompt, mod[:, :nb_p], p)
    outs_s = _sample_trunk(x_sample, mod[:, nb_p:nb_p + nb_s], p, cache_k, cache_v,
                           cache_logf, state_conv, state_ffn, page_table)
    return (outs_p[0], outs_s[0]) + outs_p[1:] + outs_s[1:]
```

```python
import functools

import jax
import jax.numpy as jnp
from jax import lax
from jax.experimental import pallas as pl
from jax.experimental.pallas import tpu as pltpu

F32 = jnp.float32
BF16 = jnp.bfloat16
EPS = 1e-6
NEG_INF = -1e30

V7X_VMEM_BYTES = 64 * 1024 * 1024
VMEM_LIMIT = V7X_VMEM_BYTES - 8 * 1024 * 1024
LANES = 128
SUBLANES = 8

MM_ROWS = 1024
MM_ROWS_BIG_K = 512
MM_COLS = 512
MM_COLS_BIG_K = 256
MM_SMALL_K = 4096
NORM_ROWS = 256
CONV_BLOCK_ROWS = 256
LOGF_ROWS = 512
ATTN_ROWS = 512
ATTN_HEADS = 2
FFN_SMALL_COLS = 5504
ADA_COLS = 512
CAST_ROWS = 688
CUMSUM_PAGES = 8
DECODE_PAGES = 2


def _pick(dim, pref, align):
    if dim <= pref:
        return dim
    t = (pref // align) * align
    while t >= align:
        if dim % t == 0:
            return t
        t -= align
    return dim


def _params(n_axes):
    return pltpu.CompilerParams(
        dimension_semantics=("arbitrary",) * n_axes,
        vmem_limit_bytes=VMEM_LIMIT)


def _silu(x):
    return x * jax.nn.sigmoid(x)


def _cumsum_rows(x):
    rows = x.shape[0]
    idx = lax.broadcasted_iota(jnp.int32, x.shape, 0)
    s = 1
    while s < rows:
        x = x + jnp.where(idx >= s, pltpu.roll(x, s, 0), 0.0)
        s *= 2
    return x


def _ada_kernel(c_ref, w_ref, b_ref, o_ref):
    cs = _silu(c_ref[...]).astype(BF16)
    o_ref[...] = jnp.dot(cs, w_ref[...].astype(BF16),
                         preferred_element_type=F32) + b_ref[...]


def _ada(c, w_ada, b_ada):
    depth, d, n = w_ada.shape
    rows = c.shape[0]
    tn = _pick(n, ADA_COLS, LANES)
    return pl.pallas_call(
        _ada_kernel,
        grid=(depth, n // tn),
        in_specs=[pl.BlockSpec((rows, d), lambda i, j: (0, 0)),
                  pl.BlockSpec((None, d, tn), lambda i, j: (i, 0, j)),
                  pl.BlockSpec((None, 1, tn), lambda i, j: (i, 0, j))],
        out_specs=pl.BlockSpec((None, rows, tn), lambda i, j: (i, 0, j)),
        out_shape=jax.ShapeDtypeStruct((depth, rows, n), F32),
        compiler_params=_params(2),
        name="ada",
    )(c, w_ada, b_ada.reshape(depth, 1, n))


def _norm_mod_kernel(x_ref, g_ref, sh_ref, sc_ref, o_ref):
    x = x_ref[...]
    y = x * lax.rsqrt(jnp.mean(x * x, axis=-1, keepdims=True) + EPS) * g_ref[...]
    o_ref[...] = (y * (1.0 + sc_ref[...]) + sh_ref[...]).astype(o_ref.dtype)


def _norm_kernel(x_ref, g_ref, o_ref):
    x = x_ref[...]
    o_ref[...] = x * lax.rsqrt(jnp.mean(x * x, axis=-1, keepdims=True) + EPS) * g_ref[...]


def _norm_mod(x, g, shift, scale):
    b, t, d = x.shape
    tt = _pick(t, NORM_ROWS, 16)
    return pl.pallas_call(
        _norm_mod_kernel,
        grid=(b, t // tt),
        in_specs=[pl.BlockSpec((None, tt, d), lambda i, j: (i, j, 0)),
                  pl.BlockSpec((1, d), lambda i, j: (0, 0)),
                  pl.BlockSpec((None, 1, d), lambda i, j: (i, 0, 0)),
                  pl.BlockSpec((None, 1, d), lambda i, j: (i, 0, 0))],
        out_specs=pl.BlockSpec((None, tt, d), lambda i, j: (i, j, 0)),
        out_shape=jax.ShapeDtypeStruct((b, t, d), BF16),
        compiler_params=_params(2),
        name="norm_mod",
    )(x, g.reshape(1, d), shift.reshape(b, 1, d), scale.reshape(b, 1, d))


def _final_norm(x, g):
    b, t, d = x.shape
    tt = _pick(t, NORM_ROWS, SUBLANES)
    return pl.pallas_call(
        _norm_kernel,
        grid=(b, t // tt),
        in_specs=[pl.BlockSpec((None, tt, d), lambda i, j: (i, j, 0)),
                  pl.BlockSpec((1, d), lambda i, j: (0, 0))],
        out_specs=pl.BlockSpec((None, tt, d), lambda i, j: (i, j, 0)),
        out_shape=jax.ShapeDtypeStruct((b, t, d), F32),
        compiler_params=_params(2),
        name="final_norm",
    )(x, g.reshape(1, d))


def _cast_once(w_ref, wb_ref):
    @pl.when(pl.program_id(1) == 0)
    def _():
        wb_ref[...] = w_ref[...].astype(BF16)


def _mm_plain_kernel(a_ref, w_ref, *rest, bf16_scale):
    out_refs, wb_ref = rest[:-1], rest[-1]
    _cast_once(w_ref, wb_ref)
    acc = jnp.dot(a_ref[...], wb_ref[...], preferred_element_type=F32)
    for o_ref in out_refs:
        if o_ref.dtype == BF16:
            o_ref[...] = (acc * bf16_scale).astype(BF16)
        else:
            o_ref[...] = acc


def _mm_res_bf16w_kernel(a_ref, w_ref, x_ref, g_ref, o_ref):
    acc = jnp.dot(a_ref[...], w_ref[...], preferred_element_type=F32)
    o_ref[...] = x_ref[...] + g_ref[...] * acc


def _cast_kernel(x_ref, o_ref):
    o_ref[...] = x_ref[...].astype(o_ref.dtype)


def _mm_glu_kernel(a_ref, wa_ref, wg_ref, o_ref, wab_ref, wgb_ref):
    _cast_once(wa_ref, wab_ref)
    _cast_once(wg_ref, wgb_ref)
    a = a_ref[...]
    va = jnp.dot(a, wab_ref[...], preferred_element_type=F32)
    vg = jnp.dot(a, wgb_ref[...], preferred_element_type=F32)
    o_ref[...] = va * jax.nn.sigmoid(vg)


def _mm_res_kernel(a_ref, w_ref, x_ref, g_ref, o_ref, wb_ref):
    _cast_once(w_ref, wb_ref)
    acc = jnp.dot(a_ref[...], wb_ref[...], preferred_element_type=F32)
    o_ref[...] = x_ref[...] + g_ref[...] * acc


def _mm_tiles(m, k):
    tm = _pick(m, MM_ROWS if k <= MM_SMALL_K else MM_ROWS_BIG_K, 16)
    tn_pref = MM_COLS if k <= MM_SMALL_K else MM_COLS_BIG_K
    return tm, tn_pref


def _mm_plain(a, w, layer, col0, ncols, out_dtypes=(F32,), bf16_scale=1.0):
    m, k = a.shape
    tm, tn_pref = _mm_tiles(m, k)
    tn = _pick(ncols, tn_pref, LANES)
    nb0 = col0 // tn
    assert col0 % tn == 0
    kern = functools.partial(_mm_plain_kernel, bf16_scale=bf16_scale)
    return pl.pallas_call(
        kern,
        grid=(ncols // tn, m // tm),
        in_specs=[pl.BlockSpec((tm, k), lambda n, i: (i, 0)),
                  pl.BlockSpec((None, k, tn), lambda n, i: (layer, 0, n + nb0))],
        out_specs=[pl.BlockSpec((tm, tn), lambda n, i: (i, n)) for _ in out_dtypes],
        out_shape=[jax.ShapeDtypeStruct((m, ncols), dt) for dt in out_dtypes],
        scratch_shapes=[pltpu.VMEM((k, tn), BF16)],
        compiler_params=_params(2),
        name="mm_plain",
    )(a, w)


def _cast_bf16(w):
    nl, k, n = w.shape
    tk = _pick(k, CAST_ROWS, 16)
    return pl.pallas_call(
        _cast_kernel,
        grid=(nl, k // tk),
        in_specs=[pl.BlockSpec((None, tk, n), lambda l, i: (l, i, 0))],
        out_specs=pl.BlockSpec((None, tk, n), lambda l, i: (l, i, 0)),
        out_shape=jax.ShapeDtypeStruct((nl, k, n), BF16),
        compiler_params=_params(2),
        name="cast_bf16",
    )(w)


def _mm_res_bf16w(a, w_bf, layer, x, gate, rows_per_gate):
    m, k = a.shape
    n_out = w_bf.shape[2]
    tm = _pick(m, MM_ROWS_BIG_K, 16)
    tn = _pick(n_out, MM_COLS, LANES)
    r = gate.shape[1]
    assert rows_per_gate % tm == 0 and r in (1, tm)
    return pl.pallas_call(
        _mm_res_bf16w_kernel,
        grid=(n_out // tn, m // tm),
        in_specs=[pl.BlockSpec((tm, k), lambda n, i: (i, 0)),
                  pl.BlockSpec((None, k, tn), lambda n, i: (layer, 0, n)),
                  pl.BlockSpec((tm, tn), lambda n, i: (i, n)),
                  pl.BlockSpec((None, r, tn),
                               lambda n, i: ((i * tm) // rows_per_gate, 0, n))],
        out_specs=pl.BlockSpec((tm, tn), lambda n, i: (i, n)),
        out_shape=jax.ShapeDtypeStruct((m, n_out), F32),
        compiler_params=_params(2),
        name="mm_res_bf16w",
    )(a, w_bf, x, gate)


def _mm_glu(a, w, layer):
    m, k = a.shape
    d = w.shape[2] // 2
    tm, tn_pref = _mm_tiles(m, k)
    tn = _pick(d, tn_pref // 2, LANES)
    nbg = d // tn
    return pl.pallas_call(
        _mm_glu_kernel,
        grid=(d // tn, m // tm),
        in_specs=[pl.BlockSpec((tm, k), lambda n, i: (i, 0)),
                  pl.BlockSpec((None, k, tn), lambda n, i: (layer, 0, n)),
                  pl.BlockSpec((None, k, tn), lambda n, i: (layer, 0, n + nbg))],
        out_specs=pl.BlockSpec((tm, tn), lambda n, i: (i, n)),
        out_shape=jax.ShapeDtypeStruct((m, d), F32),
        scratch_shapes=[pltpu.VMEM((k, tn), BF16), pltpu.VMEM((k, tn), BF16)],
        compiler_params=_params(2),
        name="mm_glu",
    )(a, w, w)


def _mm_res(a, w, layer, x, gate, rows_per_gate):
    m, k = a.shape
    n_out = w.shape[2]
    tm, tn_pref = _mm_tiles(m, k)
    tn = _pick(n_out, tn_pref, LANES)
    r = gate.shape[1]
    assert rows_per_gate % tm == 0 and r in (1, tm)
    return pl.pallas_call(
        _mm_res_kernel,
        grid=(n_out // tn, m // tm),
        in_specs=[pl.BlockSpec((tm, k), lambda n, i: (i, 0)),
                  pl.BlockSpec((None, k, tn), lambda n, i: (layer, 0, n)),
                  pl.BlockSpec((tm, tn), lambda n, i: (i, n)),
                  pl.BlockSpec((None, r, tn),
                               lambda n, i: ((i * tm) // rows_per_gate, 0, n))],
        out_specs=pl.BlockSpec((tm, tn), lambda n, i: (i, n)),
        out_shape=jax.ShapeDtypeStruct((m, n_out), F32),
        scratch_shapes=[pltpu.VMEM((k, tn), BF16)],
        compiler_params=_params(2),
        name="mm_res",
    )(a, w, x, gate)


def _mm_ffn_kernel(a_ref, wa_ref, wb_ref, cwa_ref, cwb_ref, cba_ref, cbb_ref,
                   o_ref, sta_ref, stb_ref, wab_ref, wbb_ref, ua_ref, ub_ref,
                   *, tm, taps, tiles_per_seq):
    _cast_once(wa_ref, wab_ref)
    _cast_once(wb_ref, wbb_ref)

    @pl.when(pl.program_id(1) % tiles_per_seq == 0)
    def _():
        ua_ref[0:SUBLANES, :] = jnp.zeros((SUBLANES, ua_ref.shape[1]), F32)
        ub_ref[0:SUBLANES, :] = jnp.zeros((SUBLANES, ub_ref.shape[1]), F32)

    a = a_ref[...]
    ua_ref[SUBLANES:SUBLANES + tm, :] = jnp.dot(a, wab_ref[...], preferred_element_type=F32)
    ub_ref[SUBLANES:SUBLANES + tm, :] = jnp.dot(a, wbb_ref[...], preferred_element_type=F32)

    def conv(u_ref, cw_ref, cb_ref):
        u = u_ref[...]
        y = cb_ref[...] + cw_ref[taps - 1:taps, :] * u[SUBLANES:, :]
        for back in range(1, taps):
            j = taps - 1 - back
            y = y + cw_ref[j:j + 1, :] * pltpu.roll(u, back, 0)[SUBLANES:, :]
        return y

    ya = conv(ua_ref, cwa_ref, cba_ref)
    yb = conv(ub_ref, cwb_ref, cbb_ref)
    o_ref[...] = (_silu(ya) * yb).astype(o_ref.dtype)

    keep = taps - 1
    sta_ref[...] = ua_ref[SUBLANES + tm - keep:SUBLANES + tm, :]
    stb_ref[...] = ub_ref[SUBLANES + tm - keep:SUBLANES + tm, :]
    ua_ref[0:SUBLANES, :] = ua_ref[tm:tm + SUBLANES, :]
    ub_ref[0:SUBLANES, :] = ub_ref[tm:tm + SUBLANES, :]


def _mm_ffn(a, w_up, cw, cb, layer, seq_len):
    m, k = a.shape
    f = w_up.shape[2] // 2
    taps = cw.shape[1]
    assert taps - 1 <= SUBLANES
    tm = _pick(seq_len, MM_ROWS, 16)
    tn = _pick(f, MM_COLS // 2, LANES)
    nbh = f // tn
    tiles_per_seq = seq_len // tm
    nseq = m // seq_len
    depth = cb.shape[0]
    cb3 = cb.reshape(depth, 1, 2 * f)
    kern = functools.partial(_mm_ffn_kernel, tm=tm, taps=taps, tiles_per_seq=tiles_per_seq)
    return pl.pallas_call(
        kern,
        grid=(f // tn, m // tm),
        in_specs=[pl.BlockSpec((tm, k), lambda n, i: (i, 0)),
                  pl.BlockSpec((None, k, tn), lambda n, i: (layer, 0, n)),
                  pl.BlockSpec((None, k, tn), lambda n, i: (layer, 0, n + nbh)),
                  pl.BlockSpec((None, taps, tn), lambda n, i: (layer, 0, n)),
                  pl.BlockSpec((None, taps, tn), lambda n, i: (layer, 0, n + nbh)),
                  pl.BlockSpec((None, 1, tn), lambda n, i: (layer, 0, n)),
                  pl.BlockSpec((None, 1, tn), lambda n, i: (layer, 0, n + nbh))],
        out_specs=[pl.BlockSpec((tm, tn), lambda n, i: (i, n)),
                   pl.BlockSpec((None, taps - 1, tn), lambda n, i: (i // tiles_per_seq, 0, n)),
                   pl.BlockSpec((None, taps - 1, tn), lambda n, i: (i // tiles_per_seq, 0, n))],
        out_shape=[jax.ShapeDtypeStruct((m, f), BF16),
                   jax.ShapeDtypeStruct((nseq, taps - 1, f), F32),
                   jax.ShapeDtypeStruct((nseq, taps - 1, f), F32)],
        scratch_shapes=[pltpu.VMEM((k, tn), BF16), pltpu.VMEM((k, tn), BF16),
                        pltpu.VMEM((SUBLANES + tm, tn), F32),
                        pltpu.VMEM((SUBLANES + tm, tn), F32)],
        compiler_params=_params(2),
        name="mm_ffn",
    )(a, w_up, w_up, cw, cw, cb3, cb3)


def _ffn_small_kernel(xa_ref, xb_ref, cwa_ref, cwb_ref, cba_ref, cbb_ref, o_ref, *, t, taps):
    def conv(x_ref, cw_ref, cb_ref):
        y = cb_ref[...][None]
        for j in range(taps):
            y = y + cw_ref[j:j + 1, :][None] * x_ref[:, j:j + t, :]
        return y

    ya = conv(xa_ref, cwa_ref, cba_ref)
    yb = conv(xb_ref, cwb_ref, cbb_ref)
    o_ref[...] = (_silu(ya) * yb).astype(o_ref.dtype)


def _ffn_small(xp, cw, cb, layer):
    b, rows, f2 = xp.shape
    f = f2 // 2
    taps = cw.shape[1]
    t = rows - (taps - 1)
    tn = _pick(f, FFN_SMALL_COLS, LANES)
    nbh = f // tn
    depth = cb.shape[0]
    cb3 = cb.reshape(depth, 1, f2)
    kern = functools.partial(_ffn_small_kernel, t=t, taps=taps)
    return pl.pallas_call(
        kern,
        grid=(f // tn,),
        in_specs=[pl.BlockSpec((b, rows, tn), lambda n: (0, 0, n)),
                  pl.BlockSpec((b, rows, tn), lambda n: (0, 0, n + nbh)),
                  pl.BlockSpec((None, taps, tn), lambda n: (layer, 0, n)),
                  pl.BlockSpec((None, taps, tn), lambda n: (layer, 0, n + nbh)),
                  pl.BlockSpec((None, 1, tn), lambda n: (layer, 0, n)),
                  pl.BlockSpec((None, 1, tn), lambda n: (layer, 0, n + nbh))],
        out_specs=pl.BlockSpec((b, t, tn), lambda n: (0, 0, n)),
        out_shape=jax.ShapeDtypeStruct((b, t, f), BF16),
        compiler_params=_params(1),
        name="ffn_small",
    )(xp, xp, cw, cw, cb3, cb3)


CONV_ROWS = 16
CONV_LANES = 512


def _ln_silu(y, g, b):
    mu = jnp.mean(y, axis=-1, keepdims=True)
    yc = y - mu
    var = jnp.mean(yc * yc, axis=-1, keepdims=True)
    return _silu(yc * lax.rsqrt(var + EPS) * g + b)


def _conv_ln_kernel(x_ref, w_ref, b_ref, g_ref, be_ref, o_ref, buf_ref, y_ref,
                    *, tt, taps, halo):
    d = x_ref.shape[1]

    @pl.when(pl.program_id(1) == 0)
    def _():
        buf_ref[0:halo, :] = jnp.zeros((halo, d), F32)

    buf_ref[halo:halo + tt, :] = x_ref[...]
    base = halo - (taps - 1)
    lane_chunk = min(CONV_LANES, d)
    win_rows = halo + CONV_ROWS

    def rows_step(r, carry):
        r0 = pl.multiple_of(r * CONV_ROWS, CONV_ROWS)
        for c0 in range(0, d, lane_chunk):
            win = buf_ref[pl.ds(r0, win_rows), c0:c0 + lane_chunk]
            bias = jnp.broadcast_to(b_ref[:, c0:c0 + lane_chunk], (SUBLANES, lane_chunk))
            accs = [bias] * (CONV_ROWS // SUBLANES)
            for s in range(SUBLANES):
                group = [j for j in range(taps) if (base + j) % SUBLANES == s]
                if not group:
                    continue
                shifted = win if s == 0 else pltpu.roll(win, win_rows - s, 0)
                for j in group:
                    q0 = base + j - s
                    w8 = w_ref[j, :, c0:c0 + lane_chunk]
                    accs = [acc + w8 * shifted[q0 + SUBLANES * i:q0 + SUBLANES * (i + 1), :]
                            for i, acc in enumerate(accs)]
            for i, acc in enumerate(accs):
                y_ref[SUBLANES * i:SUBLANES * (i + 1), c0:c0 + lane_chunk] = acc
        o_ref[pl.ds(r0, CONV_ROWS), :] = _ln_silu(
            y_ref[...], g_ref[...], be_ref[...]).astype(o_ref.dtype)
        return carry

    lax.fori_loop(0, tt // CONV_ROWS, rows_step, 0)
    buf_ref[0:halo, :] = buf_ref[tt:tt + halo, :]


def _conv_ln(x, w_dw, b_dw, ln_g, ln_b, layer):
    b, t, d = x.shape
    nl, taps, _ = w_dw.shape
    halo = -(-(taps - 1) // SUBLANES) * SUBLANES
    tt = _pick(t, CONV_BLOCK_ROWS, CONV_ROWS)
    assert tt % CONV_ROWS == 0 and tt >= halo
    kern = functools.partial(_conv_ln_kernel, tt=tt, taps=taps, halo=halo)
    vec = lambda a: a.reshape(nl, 1, d)
    vspec = pl.BlockSpec((None, 1, d), lambda i, j: (layer, 0, 0))
    w_rep = jnp.broadcast_to(w_dw[:, :, None, :], (nl, taps, SUBLANES, d))
    return pl.pallas_call(
        kern,
        grid=(b, t // tt),
        in_specs=[pl.BlockSpec((None, tt, d), lambda i, j: (i, j, 0)),
                  pl.BlockSpec((None, taps, SUBLANES, d), lambda i, j: (layer, 0, 0, 0)),
                  vspec, vspec, vspec],
        out_specs=pl.BlockSpec((None, tt, d), lambda i, j: (i, j, 0)),
        out_shape=jax.ShapeDtypeStruct((b, t, d), BF16),
        scratch_shapes=[pltpu.VMEM((halo + tt, d), F32),
                        pltpu.VMEM((CONV_ROWS, d), F32)],
        compiler_params=_params(2),
        name="conv_ln",
    )(x, w_rep, vec(b_dw), vec(ln_g), vec(ln_b))


def _conv_small_kernel(xp_ref, w_ref, b_ref, g_ref, be_ref, o_ref, *, t, taps):
    acc = jnp.broadcast_to(b_ref[...], (t, xp_ref.shape[1]))
    for j in range(taps):
        acc = acc + w_ref[j:j + 1, :] * xp_ref[j:j + t, :]
    o_ref[...] = _ln_silu(acc, g_ref[...], be_ref[...]).astype(o_ref.dtype)


def _conv_small(xp, w_dw, b_dw, ln_g, ln_b, layer):
    b, rows, d = xp.shape
    nl, taps, _ = w_dw.shape
    t = rows - (taps - 1)
    kern = functools.partial(_conv_small_kernel, t=t, taps=taps)
    vec = lambda a: a.reshape(nl, 1, d)
    vspec = pl.BlockSpec((None, 1, d), lambda i: (layer, 0, 0))
    return pl.pallas_call(
        kern,
        grid=(b,),
        in_specs=[pl.BlockSpec((None, rows, d), lambda i: (i, 0, 0)),
                  pl.BlockSpec((None, taps, d), lambda i: (layer, 0, 0)),
                  vspec, vspec, vspec],
        out_specs=pl.BlockSpec((None, t, d), lambda i: (i, 0, 0)),
        out_shape=jax.ShapeDtypeStruct((b, t, d), BF16),
        compiler_params=_params(1),
        name="conv_small",
    )(xp, w_dw, vec(b_dw), vec(ln_g), vec(ln_b))


def _log_sigmoid(z):
    return jnp.minimum(z, 0.0) - jnp.log1p(jnp.exp(-jnp.abs(z)))


def _logf_kernel(a_ref, w_ref, b_ref, lf_ref, nf_ref, carry_ref, *, tiles_per_seq):
    @pl.when(pl.program_id(0) % tiles_per_seq == 0)
    def _():
        carry_ref[...] = jnp.zeros(carry_ref.shape, F32)

    z = jnp.dot(a_ref[...], w_ref[...].astype(BF16), preferred_element_type=F32) + b_ref[...]
    lf = _log_sigmoid(z)
    lf_ref[...] = lf
    c = _cumsum_rows(lf) + carry_ref[...]
    nf_ref[...] = -c
    tm = lf.shape[0]
    carry_ref[...] = c[tm - 1:tm, :]


def _logf(a, w_f, b_f, seq_len):
    m, k = a.shape
    tm = _pick(seq_len, LOGF_ROWS, 16)
    kern = functools.partial(_logf_kernel, tiles_per_seq=seq_len // tm)
    return pl.pallas_call(
        kern,
        grid=(m // tm,),
        in_specs=[pl.BlockSpec((tm, k), lambda i: (i, 0)),
                  pl.BlockSpec((k, LANES), lambda i: (0, 0)),
                  pl.BlockSpec((1, LANES), lambda i: (0, 0))],
        out_specs=[pl.BlockSpec((tm, LANES), lambda i: (i, 0)),
                   pl.BlockSpec((tm, LANES), lambda i: (i, 0))],
        out_shape=[jax.ShapeDtypeStruct((m, LANES), F32),
                   jax.ShapeDtypeStruct((m, LANES), F32)],
        scratch_shapes=[pltpu.VMEM((1, LANES), F32)],
        compiler_params=_params(1),
        name="logf",
    )(a, w_f, b_f)


def _flash_step(s, v, m_ref, l_ref, acc_ref):
    m_prev = m_ref[...]
    m_new = jnp.maximum(m_prev, jnp.max(s, axis=1, keepdims=True))
    alpha = jnp.exp(m_prev - m_new)
    p = jnp.exp(s - m_new)
    l_ref[...] = alpha * l_ref[...] + jnp.sum(p, axis=1, keepdims=True)
    acc_ref[...] = alpha * acc_ref[...] + jnp.dot(
        p.astype(BF16), v, preferred_element_type=F32)
    m_ref[...] = m_new


def _attn_kernel(q_ref, k_ref, v_ref, nf_ref, o_ref, kt_ref, va_ref, m_ref, acc_ref,
                 *, tq, dh, heads):
    qi = pl.program_id(2)
    t = k_ref.shape[0]

    @pl.when(qi == 0)
    def _():
        for h in range(heads):
            lanes = slice(h * dh, (h + 1) * dh)
            for c0 in range(0, t, tq):
                kt_ref[h, :, c0:c0 + tq] = k_ref[c0:c0 + tq, lanes].astype(F32).T.astype(BF16)
            va_ref[h, :, 0:dh] = v_ref[:, lanes]
            va_ref[h, :, dh:2 * dh] = jnp.ones((t, dh), BF16)

    m_ref[...] = jnp.full(m_ref.shape, NEG_INF, F32)
    acc_ref[...] = jnp.zeros(acc_ref.shape, F32)

    def chunk(c, diagonal):
        k0 = pl.multiple_of(c * tq, tq)
        for h in range(heads):
            s = jnp.dot(q_ref[:, h * dh:(h + 1) * dh], kt_ref[h, :, pl.ds(k0, tq)],
                        preferred_element_type=F32)
            s = s + nf_ref[h:h + 1, pl.ds(k0, tq)]
            if diagonal:
                row = lax.broadcasted_iota(jnp.int32, s.shape, 0)
                col = lax.broadcasted_iota(jnp.int32, s.shape, 1)
                s = jnp.where(col <= row, s, NEG_INF)
            m_prev = m_ref[h]
            m_new = jnp.maximum(m_prev, jnp.max(s, axis=1, keepdims=True))
            p = jnp.exp(s - m_new).astype(BF16)
            acc_ref[h] = jnp.exp(m_prev - m_new) * acc_ref[h] + jnp.dot(
                p, va_ref[h, pl.ds(k0, tq), :], preferred_element_type=F32)
            m_ref[h] = m_new

    def full_chunk(c, carry):
        chunk(c, False)
        return carry

    lax.fori_loop(0, qi, full_chunk, 0)
    chunk(qi, True)
    for h in range(heads):
        acc = acc_ref[h]
        o_ref[:, h * dh:(h + 1) * dh] = (acc[:, 0:dh] / acc[:, dh:2 * dh]).astype(o_ref.dtype)


def _attention(q, k, v, negf_rows, n_heads):
    b, t, d = q.shape
    dh = d // n_heads
    heads = ATTN_HEADS if n_heads % ATTN_HEADS == 0 else 1
    hw = heads * dh
    tq = _pick(t, ATTN_ROWS, LANES)
    kern = functools.partial(_attn_kernel, tq=tq, dh=dh, heads=heads)
    qspec = pl.BlockSpec((None, tq, hw), lambda bi, hp, qi: (bi, qi, hp))
    kspec = pl.BlockSpec((None, t, hw), lambda bi, hp, qi: (bi, 0, hp))
    return pl.pallas_call(
        kern,
        grid=(b, n_heads // heads, t // tq),
        in_specs=[qspec, kspec, kspec,
                  pl.BlockSpec((None, None, heads, t), lambda bi, hp, qi: (bi, hp, 0, 0))],
        out_specs=qspec,
        out_shape=jax.ShapeDtypeStruct((b, t, d), BF16),
        scratch_shapes=[pltpu.VMEM((heads, dh, t), BF16),
                        pltpu.VMEM((heads, t, 2 * dh), BF16),
                        pltpu.VMEM((heads, tq, 1), F32),
                        pltpu.VMEM((heads, tq, 2 * dh), F32)],
        compiler_params=_params(3),
        name="attention",
    )(q, k, v, negf_rows.reshape(b, n_heads // heads, heads, t))


def _page_cumsum_kernel(pt_ref, *refs, group, n_groups):
    past_refs, new_ref, o_ref, carry_ref = refs[:group], refs[group], refs[group + 1], refs[group + 2]
    p = pl.program_id(1)

    @pl.when(p == 0)
    def _():
        carry_ref[...] = jnp.zeros(carry_ref.shape, F32)

    def run(src_ref, slot):
        c = _cumsum_rows(src_ref[...]) + carry_ref[...]
        o_ref[slot] = -c
        rows = c.shape[0]
        carry_ref[...] = c[rows - 1:rows, :]

    @pl.when(p < n_groups)
    def _():
        for g in range(group):
            run(past_refs[g], g)

    @pl.when(p == n_groups)
    def _():
        run(new_ref, 0)
        for g in range(1, group):
            o_ref[g] = jnp.zeros(o_ref.shape[1:], F32)


def _page_cumsum(page_table, cache_logf, logf_new_pad, layer):
    nb, n_pages = page_table.shape
    _, _, page, h = cache_logf.shape
    group = _pick(n_pages, CUMSUM_PAGES, 1)
    n_groups = n_pages // group
    kern = functools.partial(_page_cumsum_kernel, group=group, n_groups=n_groups)

    def past_spec(g):
        return pl.BlockSpec(
            (None, None, page, h),
            lambda bi, p, pt: (layer, pt[bi * n_pages + jnp.minimum(p, n_groups - 1) * group + g], 0, 0))

    grid_spec = pltpu.PrefetchScalarGridSpec(
        num_scalar_prefetch=1,
        grid=(nb, n_groups + 1),
        in_specs=[past_spec(g) for g in range(group)]
        + [pl.BlockSpec((None, page, h), lambda bi, p, pt: (bi, 0, 0))],
        out_specs=pl.BlockSpec((None, group, page, h), lambda bi, p, pt: (bi, p, 0, 0)),
        scratch_shapes=[pltpu.VMEM((1, h), F32)])
    out = pl.pallas_call(
        kern,
        grid_spec=grid_spec,
        out_shape=jax.ShapeDtypeStruct((nb, (n_groups + 1) * group, page, h), F32),
        compiler_params=_params(2),
        name="page_cumsum",
    )(page_table.reshape(-1), *([cache_logf] * group), logf_new_pad)
    return out[:, :n_pages + 1]


def _decode_kernel(pt_ref, q_ref, *refs, group, n_groups, n_heads):
    kp_refs, vp_refs = refs[:group], refs[group:2 * group]
    kn_ref, vn_ref = refs[2 * group], refs[2 * group + 1]
    nf_refs = refs[2 * group + 2:3 * group + 2]
    o_ref, m_ref, l_ref, acc_ref = refs[3 * group + 2:]
    p = pl.program_id(1)

    @pl.when(p == 0)
    def _():
        m_ref[...] = jnp.full(m_ref.shape, NEG_INF, F32)
        l_ref[...] = jnp.zeros(l_ref.shape, F32)
        acc_ref[...] = jnp.zeros(acc_ref.shape, F32)

    def run(k_ref, v_ref, nf_ref, causal):
        s = lax.dot_general(q_ref[...], k_ref[...].astype(BF16),
                            (((1,), (1,)), ((), ())), preferred_element_type=F32)
        s = s + nf_ref[...]
        row = lax.broadcasted_iota(jnp.int32, s.shape, 0)
        col = lax.broadcasted_iota(jnp.int32, s.shape, 1)
        ok = (row % n_heads) == (col % n_heads)
        if causal:
            ok = ok & ((col // n_heads) <= (row // n_heads))
        s = jnp.where(ok, s, NEG_INF)
        _flash_step(s, v_ref[...].astype(BF16), m_ref, l_ref, acc_ref)

    @pl.when(p < n_groups)
    def _():
        for g in range(group):
            run(kp_refs[g], vp_refs[g], nf_refs[g], False)

    @pl.when(p == n_groups)
    def _():
        run(kn_ref, vn_ref, nf_refs[0], True)
        o_ref[...] = acc_ref[...] / l_ref[...]


def _decode_attention(page_table, q_rows, cache_k, cache_v, k_new, v_new, negf, layer, n_heads):
    nb, n_pages = page_table.shape
    _, qr, dh = q_rows.shape
    pr = cache_k.shape[2]
    group = _pick(n_pages, DECODE_PAGES, 1)
    n_groups = n_pages // group
    kern = functools.partial(_decode_kernel, group=group, n_groups=n_groups, n_heads=n_heads)

    def page_spec(g):
        return pl.BlockSpec(
            (None, None, pr, dh),
            lambda bi, p, pt: (layer, pt[bi * n_pages + jnp.minimum(p, n_groups - 1) * group + g], 0, 0))

    def bias_spec(g):
        return pl.BlockSpec((None, None, 1, pr),
                            lambda bi, p, pt: (bi, jnp.minimum(p * group + g, n_pages), 0, 0))

    new_spec = pl.BlockSpec((None, pr, dh), lambda bi, p, pt: (bi, 0, 0))
    pages = [page_spec(g) for g in range(group)]
    grid_spec = pltpu.PrefetchScalarGridSpec(
        num_scalar_prefetch=1,
        grid=(nb, n_groups + 1),
        in_specs=[pl.BlockSpec((None, qr, dh), lambda bi, p, pt: (bi, 0, 0))]
        + pages + pages + [new_spec, new_spec] + [bias_spec(g) for g in range(group)],
        out_specs=pl.BlockSpec((None, qr, dh), lambda bi, p, pt: (bi, 0, 0)),
        scratch_shapes=[pltpu.VMEM((qr, 1), F32), pltpu.VMEM((qr, 1), F32),
                        pltpu.VMEM((qr, dh), F32)])
    return pl.pallas_call(
        kern,
        grid_spec=grid_spec,
        out_shape=jax.ShapeDtypeStruct((nb, qr, dh), F32),
        compiler_params=_params(2),
        name="decode_attention",
    )(page_table.reshape(-1), q_rows, *([cache_k] * group), *([cache_v] * group),
      k_new, v_new, *([negf] * group))


def _forget_weights(w_qkvf, b_f, layer, d, n_heads):
    w_f = jnp.pad(w_qkvf[layer, :, 3 * d:], ((0, 0), (0, LANES - n_heads)))
    bias = jnp.pad(b_f[layer], (0, LANES - n_heads)).reshape(1, LANES)
    return w_f, bias


def _prompt_trunk(x, mod, p):
    b, t, d = x.shape
    m = b * t
    depth = p['w_up'].shape[0]
    n_heads = p['b_f'].shape[1]
    x2 = x.reshape(m, d)
    new_k, new_v, new_logf, new_conv, new_ffn = [], [], [], [], []
    for i in range(depth):
        sh_m, sc_m, gt_m, sh_f, sc_f, gt_f = jnp.split(mod[i], 6, axis=-1)
        h = _norm_mod(x2.reshape(b, t, d), p['g_mix'][i], sh_m, sc_m).reshape(m, d)
        j = i // 2
        if i % 2 == 0:
            glu = _mm_glu(h, p['w_pw1'], j).reshape(b, t, d)
            taps = p['w_dw'].shape[1]
            new_conv.append(glu[:, t - (taps - 1):])
            y = _conv_ln(glu, p['w_dw'], p['b_dw'], p['ln_g'], p['ln_b'], j).reshape(m, d)
            x2 = _mm_res(y, p['w_pw2'], j, x2, gt_m.reshape(b, 1, d), t)
        else:
            scale = (d // n_heads) ** -0.5
            q_bf, = _mm_plain(h, p['w_qkvf'], j, 0, d, (BF16,), scale)
            k, k_bf = _mm_plain(h, p['w_qkvf'], j, d, d, (F32, BF16))
            v, v_bf = _mm_plain(h, p['w_qkvf'], j, 2 * d, d, (F32, BF16))
            w_f, bias_f = _forget_weights(p['w_qkvf'], p['b_f'], j, d, n_heads)
            logf, negf = _logf(h, w_f, bias_f, t)
            negf_rows = jnp.transpose(negf.reshape(b, t, LANES)[:, :, :n_heads], (0, 2, 1))
            o = _attention(q_bf.reshape(b, t, d), k_bf.reshape(b, t, d), v_bf.reshape(b, t, d),
                           negf_rows, n_heads).reshape(m, d)
            x2 = _mm_res(o, p['w_o'], j, x2, gt_m.reshape(b, 1, d), t)
            new_k.append(k.reshape(b, t, n_heads, d // n_heads))
            new_v.append(v.reshape(b, t, n_heads, d // n_heads))
            new_logf.append(logf.reshape(b, t, LANES)[:, :, :n_heads])
        h = _norm_mod(x2.reshape(b, t, d), p['g_ffn'][i], sh_f, sc_f).reshape(m, d)
        act, st_a, st_b = _mm_ffn(h, p['w_up'], p['w_ffn_dw'], p['b_ffn_dw'], i, t)
        new_ffn.append(jnp.concatenate([st_a, st_b], axis=-1))
        x2 = _mm_res_bf16w(act, p['w_down_bf'], i, x2, gt_f.reshape(b, 1, d), t)
    y = _final_norm(x2.reshape(b, t, d), p['g_out'])
    return (y, jnp.stack(new_k), jnp.stack(new_v), jnp.stack(new_logf),
            jnp.stack(new_conv), jnp.stack(new_ffn))


def _sample_trunk(x, mod, p, cache_k, cache_v, cache_logf, state_conv, state_ffn, page_table):
    b, t, d = x.shape
    m = b * t
    depth = p['w_up'].shape[0]
    n_heads = p['b_f'].shape[1]
    dh = d // n_heads
    nl, n_pool, page, _, _ = cache_k.shape
    ck = cache_k.reshape(nl, n_pool, page * n_heads, dh)
    cv = cache_v.reshape(nl, n_pool, page * n_heads, dh)
    x2 = x.reshape(m, d)
    rep = lambda g: jnp.repeat(g, t, axis=0).reshape(1, m, d)
    pad_page = lambda a: jnp.pad(a, ((0, 0), (0, page - t)) + ((0, 0),) * (a.ndim - 2))
    new_k, new_v, new_logf, new_conv, new_ffn = [], [], [], [], []
    for i in range(depth):
        sh_m, sc_m, gt_m, sh_f, sc_f, gt_f = jnp.split(mod[i], 6, axis=-1)
        h = _norm_mod(x2.reshape(b, t, d), p['g_mix'][i], sh_m, sc_m).reshape(m, d)
        j = i // 2
        if i % 2 == 0:
            glu = _mm_glu(h, p['w_pw1'], j).reshape(b, t, d)
            xp = jnp.concatenate([state_conv[j], glu], axis=1)
            taps = p['w_dw'].shape[1]
            new_conv.append(xp[:, xp.shape[1] - (taps - 1):])
            y = _conv_small(xp, p['w_dw'], p['b_dw'], p['ln_g'], p['ln_b'], j).reshape(m, d)
            x2 = _mm_res(y, p['w_pw2'], j, x2, rep(gt_m), m)
        else:
            q, = _mm_plain(h, p['w_qkvf'], j, 0, d, (BF16,), dh ** -0.5)
            k, = _mm_plain(h, p['w_qkvf'], j, d, d)
            v, = _mm_plain(h, p['w_qkvf'], j, 2 * d, d)
            w_f, bias_f = _forget_weights(p['w_qkvf'], p['b_f'], j, d, n_heads)
            logf = _logf(h, w_f, bias_f, m)[0].reshape(b, t, LANES)[:, :, :n_heads]
            negf = _page_cumsum(page_table, cache_logf, pad_page(logf), j)
            k_new = pad_page(k.reshape(b, t, n_heads, dh)).reshape(b, page * n_heads, dh)
            v_new = pad_page(v.reshape(b, t, n_heads, dh)).reshape(b, page * n_heads, dh)
            o = _decode_attention(
                page_table, q.reshape(b, t * n_heads, dh), ck, cv, k_new, v_new,
                negf.reshape(b, negf.shape[1], 1, page * n_heads), j, n_heads)
            x2 = _mm_res(o.reshape(m, d).astype(BF16), p['w_o'], j, x2, rep(gt_m), m)
            new_k.append(k.reshape(b, t, n_heads, dh))
            new_v.append(v.reshape(b, t, n_heads, dh))
            new_logf.append(logf)
        h = _norm_mod(x2.reshape(b, t, d), p['g_ffn'][i], sh_f, sc_f).reshape(m, d)
        f2 = p['w_up'].shape[2]
        u = _mm_plain(h, p['w_up'], i, 0, f2)[0].reshape(b, t, f2)
        xp = jnp.concatenate([state_ffn[i], u], axis=1)
        ftaps = p['w_ffn_dw'].shape[1]
        new_ffn.append(xp[:, xp.shape[1] - (ftaps - 1):])
        act = _ffn_small(xp, p['w_ffn_dw'], p['b_ffn_dw'], i).reshape(m, f2 // 2)
        x2 = _mm_res_bf16w(act, p['w_down_bf'], i, x2, rep(gt_f), m)
    y = _final_norm(x2.reshape(b, t, d), p['g_out'])
    return (y, jnp.stack(new_k), jnp.stack(new_v), jnp.stack(new_logf),
            jnp.stack(new_conv), jnp.stack(new_ffn))


def kernel(x_prompt, x_sample, c_prompt, c_sample, cache_k, cache_v, cache_logf, state_conv, state_ffn, page_table, w_ada, b_ada, g_mix, g_ffn, g_out, w_pw1, w_dw, b_dw, ln_g, ln_b, w_pw2, w_qkvf, b_f, w_o, w_up, w_ffn_dw, b_ffn_dw, w_down):
    p = dict(g_mix=g_mix, g_ffn=g_ffn, g_out=g_out, w_pw1=w_pw1, w_dw=w_dw, b_dw=b_dw,
             ln_g=ln_g, ln_b=ln_b, w_pw2=w_pw2, w_qkvf=w_qkvf, b_f=b_f, w_o=w_o,
             w_up=w_up, w_ffn_dw=w_ffn_dw, b_ffn_dw=b_ffn_dw, w_down_bf=_cast_bf16(w_down))
    nb_p = c_prompt.shape[0]
    nb_s = c_sample.shape[0]
    c_all = jnp.concatenate([c_prompt, c_sample], axis=0)
    pad = (-c_all.shape[0]) % 16
    c_all = jnp.pad(c_all, ((0, pad), (0, 0)))
    mod = _ada(c_all, w_ada, b_ada)
    outs_p = _prompt_trunk(x_prompt, mod[:, :nb_p], p)
    outs_s = _sample_trunk(x_sample, mod[:, nb_p:nb_p + nb_s], p, cache_k, cache_v,
                           cache_logf, state_conv, state_ffn, page_table)
    return (outs_p[0], outs_s[0]) + outs_p[1:] + outs_s[1:]
```

```python
import functools

import jax
import jax.numpy as jnp
from jax import lax
from jax.experimental import pallas as pl
from jax.experimental.pallas import tpu as pltpu

F32 = jnp.float32
BF16 = jnp.bfloat16
EPS = 1e-6
NEG_INF = -1e30

V7X_VMEM_BYTES = 64 * 1024 * 1024
VMEM_LIMIT = V7X_VMEM_BYTES - 8 * 1024 * 1024
LANES = 128
SUBLANES = 8

MM_ROWS = 1024
MM_ROWS_BIG_K = 512
MM_COLS = 512
MM_COLS_BIG_K = 256
MM_SMALL_K = 4096
NORM_ROWS = 256
CONV_BLOCK_ROWS = 256
LOGF_ROWS = 512
ATTN_ROWS = 512
ATTN_HEADS = 4
FFN_SMALL_COLS = 5504
ADA_COLS = 512
CAST_ROWS = 688
CUMSUM_PAGES = 8
DECODE_PAGES = 4


def _pick(dim, pref, align):
    if dim <= pref:
        return dim
    t = (pref // align) * align
    while t >= align:
        if dim % t == 0:
            return t
        t -= align
    return dim


def _params(n_axes):
    return pltpu.CompilerParams(
        dimension_semantics=("arbitrary",) * n_axes,
        vmem_limit_bytes=VMEM_LIMIT)


def _silu(x):
    return x * jax.nn.sigmoid(x)


def _cumsum_rows(x):
    rows = x.shape[0]
    idx = lax.broadcasted_iota(jnp.int32, x.shape, 0)
    s = 1
    while s < rows:
        x = x + jnp.where(idx >= s, pltpu.roll(x, s, 0), 0.0)
        s *= 2
    return x


def _ada_kernel(c_ref, w_ref, b_ref, o_ref):
    cs = _silu(c_ref[...]).astype(BF16)
    o_ref[...] = jnp.dot(cs, w_ref[...].astype(BF16),
                         preferred_element_type=F32) + b_ref[...]


def _ada(c, w_ada, b_ada):
    depth, d, n = w_ada.shape
    rows = c.shape[0]
    tn = _pick(n, ADA_COLS, LANES)
    return pl.pallas_call(
        _ada_kernel,
        grid=(depth, n // tn),
        in_specs=[pl.BlockSpec((rows, d), lambda i, j: (0, 0)),
                  pl.BlockSpec((None, d, tn), lambda i, j: (i, 0, j)),
                  pl.BlockSpec((None, 1, tn), lambda i, j: (i, 0, j))],
        out_specs=pl.BlockSpec((None, rows, tn), lambda i, j: (i, 0, j)),
        out_shape=jax.ShapeDtypeStruct((depth, rows, n), F32),
        compiler_params=_params(2),
        name="ada",
    )(c, w_ada, b_ada.reshape(depth, 1, n))


def _norm_mod_kernel(x_ref, g_ref, sh_ref, sc_ref, o_ref):
    x = x_ref[...]
    y = x * lax.rsqrt(jnp.mean(x * x, axis=-1, keepdims=True) + EPS) * g_ref[...]
    o_ref[...] = (y * (1.0 + sc_ref[...]) + sh_ref[...]).astype(o_ref.dtype)


def _norm_kernel(x_ref, g_ref, o_ref):
    x = x_ref[...]
    o_ref[...] = x * lax.rsqrt(jnp.mean(x * x, axis=-1, keepdims=True) + EPS) * g_ref[...]


def _norm_mod(x, g, shift, scale):
    b, t, d = x.shape
    tt = _pick(t, NORM_ROWS, 16)
    return pl.pallas_call(
        _norm_mod_kernel,
        grid=(b, t // tt),
        in_specs=[pl.BlockSpec((None, tt, d), lambda i, j: (i, j, 0)),
                  pl.BlockSpec((1, d), lambda i, j: (0, 0)),
                  pl.BlockSpec((None, 1, d), lambda i, j: (i, 0, 0)),
                  pl.BlockSpec((None, 1, d), lambda i, j: (i, 0, 0))],
        out_specs=pl.BlockSpec((None, tt, d), lambda i, j: (i, j, 0)),
        out_shape=jax.ShapeDtypeStruct((b, t, d), BF16),
        compiler_params=_params(2),
        name="norm_mod",
    )(x, g.reshape(1, d), shift.reshape(b, 1, d), scale.reshape(b, 1, d))


def _final_norm(x, g):
    b, t, d = x.shape
    tt = _pick(t, NORM_ROWS, SUBLANES)
    return pl.pallas_call(
        _norm_kernel,
        grid=(b, t // tt),
        in_specs=[pl.BlockSpec((None, tt, d), lambda i, j: (i, j, 0)),
                  pl.BlockSpec((1, d), lambda i, j: (0, 0))],
        out_specs=pl.BlockSpec((None, tt, d), lambda i, j: (i, j, 0)),
        out_shape=jax.ShapeDtypeStruct((b, t, d), F32),
        compiler_params=_params(2),
        name="final_norm",
    )(x, g.reshape(1, d))


def _cast_once(w_ref, wb_ref):
    @pl.when(pl.program_id(1) == 0)
    def _():
        wb_ref[...] = w_ref[...].astype(BF16)


def _mm_plain_kernel(a_ref, w_ref, *rest, bf16_scale):
    out_refs, wb_ref = rest[:-1], rest[-1]
    _cast_once(w_ref, wb_ref)
    acc = jnp.dot(a_ref[...], wb_ref[...], preferred_element_type=F32)
    for o_ref in out_refs:
        if o_ref.dtype == BF16:
            o_ref[...] = (acc * bf16_scale).astype(BF16)
        else:
            o_ref[...] = acc


def _mm_proj_kernel(a_ref, wt_ref, *rest, bf16_scale, n_prev, want_f32, want_bf16):
    rest = list(rest)
    wb_ref = rest.pop()
    prev_ref = rest.pop(0) if n_prev else None
    _cast_once(wt_ref, wb_ref)
    acc = lax.dot_general(a_ref[...], wb_ref[...], (((1,), (1,)), ((), ())),
                          preferred_element_type=F32)
    if want_f32:
        stack_ref = rest.pop(0)
        if n_prev:
            stack_ref[0:n_prev] = prev_ref[...]
        stack_ref[n_prev] = acc
    if want_bf16:
        rest.pop(0)[...] = (acc * bf16_scale).astype(BF16)


def _mm_res_bf16w_kernel(a_ref, w_ref, x_ref, g_ref, o_ref):
    acc = jnp.dot(a_ref[...], w_ref[...], preferred_element_type=F32)
    o_ref[...] = x_ref[...] + g_ref[...] * acc


def _cast_kernel(x_ref, o_ref):
    o_ref[...] = x_ref[...].astype(o_ref.dtype)


def _mm_glu_kernel(a_ref, wa_ref, wg_ref, o_ref, wab_ref, wgb_ref):
    _cast_once(wa_ref, wab_ref)
    _cast_once(wg_ref, wgb_ref)
    a = a_ref[...]
    va = jnp.dot(a, wab_ref[...], preferred_element_type=F32)
    vg = jnp.dot(a, wgb_ref[...], preferred_element_type=F32)
    o_ref[...] = va * jax.nn.sigmoid(vg)


def _mm_res_kernel(a_ref, w_ref, x_ref, g_ref, o_ref, wb_ref):
    _cast_once(w_ref, wb_ref)
    acc = jnp.dot(a_ref[...], wb_ref[...], preferred_element_type=F32)
    o_ref[...] = x_ref[...] + g_ref[...] * acc


def _mm_tiles(m, k):
    tm = _pick(m, MM_ROWS if k <= MM_SMALL_K else MM_ROWS_BIG_K, 16)
    tn_pref = MM_COLS if k <= MM_SMALL_K else MM_COLS_BIG_K
    return tm, tn_pref


def _mm_plain(a, w, layer, col0, ncols, out_dtypes=(F32,), bf16_scale=1.0):
    m, k = a.shape
    tm, tn_pref = _mm_tiles(m, k)
    tn = _pick(ncols, tn_pref, LANES)
    nb0 = col0 // tn
    assert col0 % tn == 0
    kern = functools.partial(_mm_plain_kernel, bf16_scale=bf16_scale)
    return pl.pallas_call(
        kern,
        grid=(ncols // tn, m // tm),
        in_specs=[pl.BlockSpec((tm, k), lambda n, i: (i, 0)),
                  pl.BlockSpec((None, k, tn), lambda n, i: (layer, 0, n + nb0))],
        out_specs=[pl.BlockSpec((tm, tn), lambda n, i: (i, n)) for _ in out_dtypes],
        out_shape=[jax.ShapeDtypeStruct((m, ncols), dt) for dt in out_dtypes],
        scratch_shapes=[pltpu.VMEM((k, tn), BF16)],
        compiler_params=_params(2),
        name="mm_plain",
    )(a, w)


def _mm_proj(a, wt, layer, col0, ncols, prev=None, want_f32=True, want_bf16=False,
             bf16_scale=1.0):
    m, k = a.shape
    tm, tn_pref = _mm_tiles(m, k)
    tn = _pick(ncols, tn_pref, LANES)
    nb0 = col0 // tn
    assert col0 % tn == 0
    n_prev = 0 if prev is None else prev.shape[0]
    kern = functools.partial(_mm_proj_kernel, bf16_scale=bf16_scale, n_prev=n_prev,
                             want_f32=want_f32, want_bf16=want_bf16)
    in_specs = [pl.BlockSpec((tm, k), lambda n, i: (i, 0)),
                pl.BlockSpec((None, tn, k), lambda n, i: (layer, n + nb0, 0))]
    operands = [a, wt]
    if n_prev:
        in_specs.append(pl.BlockSpec((n_prev, tm, tn), lambda n, i: (0, i, n)))
        operands.append(prev)
    out_specs, out_shape = [], []
    if want_f32:
        out_specs.append(pl.BlockSpec((n_prev + 1, tm, tn), lambda n, i: (0, i, n)))
        out_shape.append(jax.ShapeDtypeStruct((n_prev + 1, m, ncols), F32))
    if want_bf16:
        out_specs.append(pl.BlockSpec((tm, tn), lambda n, i: (i, n)))
        out_shape.append(jax.ShapeDtypeStruct((m, ncols), BF16))
    return pl.pallas_call(
        kern,
        grid=(ncols // tn, m // tm),
        in_specs=in_specs,
        out_specs=out_specs,
        out_shape=out_shape,
        scratch_shapes=[pltpu.VMEM((tn, k), BF16)],
        compiler_params=_params(2),
        name="mm_proj",
    )(*operands)


def _cast_bf16(w):
    nl, k, n = w.shape
    tk = _pick(k, CAST_ROWS, 16)
    return pl.pallas_call(
        _cast_kernel,
        grid=(nl, k // tk),
        in_specs=[pl.BlockSpec((None, tk, n), lambda l, i: (l, i, 0))],
        out_specs=pl.BlockSpec((None, tk, n), lambda l, i: (l, i, 0)),
        out_shape=jax.ShapeDtypeStruct((nl, k, n), BF16),
        compiler_params=_params(2),
        name="cast_bf16",
    )(w)


def _mm_res_bf16w(a, w_bf, layer, x, gate, rows_per_gate):
    m, k = a.shape
    n_out = w_bf.shape[2]
    tm = _pick(m, MM_ROWS_BIG_K, 16)
    tn = _pick(n_out, MM_COLS, LANES)
    r = gate.shape[1]
    assert rows_per_gate % tm == 0 and r in (1, tm)
    return pl.pallas_call(
        _mm_res_bf16w_kernel,
        grid=(n_out // tn, m // tm),
        in_specs=[pl.BlockSpec((tm, k), lambda n, i: (i, 0)),
                  pl.BlockSpec((None, k, tn), lambda n, i: (layer, 0, n)),
                  pl.BlockSpec((tm, tn), lambda n, i: (i, n)),
                  pl.BlockSpec((None, r, tn),
                               lambda n, i: ((i * tm) // rows_per_gate, 0, n))],
        out_specs=pl.BlockSpec((tm, tn), lambda n, i: (i, n)),
        out_shape=jax.ShapeDtypeStruct((m, n_out), F32),
        compiler_params=_params(2),
        name="mm_res_bf16w",
    )(a, w_bf, x, gate)


def _mm_glu(a, w, layer):
    m, k = a.shape
    d = w.shape[2] // 2
    tm, tn_pref = _mm_tiles(m, k)
    tn = _pick(d, tn_pref // 2, LANES)
    nbg = d // tn
    return pl.pallas_call(
        _mm_glu_kernel,
        grid=(d // tn, m // tm),
        in_specs=[pl.BlockSpec((tm, k), lambda n, i: (i, 0)),
                  pl.BlockSpec((None, k, tn), lambda n, i: (layer, 0, n)),
                  pl.BlockSpec((None, k, tn), lambda n, i: (layer, 0, n + nbg))],
        out_specs=pl.BlockSpec((tm, tn), lambda n, i: (i, n)),
        out_shape=jax.ShapeDtypeStruct((m, d), F32),
        scratch_shapes=[pltpu.VMEM((k, tn), BF16), pltpu.VMEM((k, tn), BF16)],
        compiler_params=_params(2),
        name="mm_glu",
    )(a, w, w)


def _mm_res(a, w, layer, x, gate, rows_per_gate):
    m, k = a.shape
    n_out = w.shape[2]
    tm, tn_pref = _mm_tiles(m, k)
    tn = _pick(n_out, tn_pref, LANES)
    r = gate.shape[1]
    assert rows_per_gate % tm == 0 and r in (1, tm)
    return pl.pallas_call(
        _mm_res_kernel,
        grid=(n_out // tn, m // tm),
        in_specs=[pl.BlockSpec((tm, k), lambda n, i: (i, 0)),
                  pl.BlockSpec((None, k, tn), lambda n, i: (layer, 0, n)),
                  pl.BlockSpec((tm, tn), lambda n, i: (i, n)),
                  pl.BlockSpec((None, r, tn),
                               lambda n, i: ((i * tm) // rows_per_gate, 0, n))],
        out_specs=pl.BlockSpec((tm, tn), lambda n, i: (i, n)),
        out_shape=jax.ShapeDtypeStruct((m, n_out), F32),
        scratch_shapes=[pltpu.VMEM((k, tn), BF16)],
        compiler_params=_params(2),
        name="mm_res",
    )(a, w, x, gate)


def _mm_ffn_kernel(a_ref, wa_ref, wb_ref, cwa_ref, cwb_ref, cba_ref, cbb_ref,
                   o_ref, sta_ref, stb_ref, wab_ref, wbb_ref, ua_ref, ub_ref,
                   *, tm, taps, tiles_per_seq):
    _cast_once(wa_ref, wab_ref)
    _cast_once(wb_ref, wbb_ref)

    @pl.when(pl.program_id(1) % tiles_per_seq == 0)
    def _():
        ua_ref[0:SUBLANES, :] = jnp.zeros((SUBLANES, ua_ref.shape[1]), F32)
        ub_ref[0:SUBLANES, :] = jnp.zeros((SUBLANES, ub_ref.shape[1]), F32)

    a = a_ref[...]
    ua_ref[SUBLANES:SUBLANES + tm, :] = jnp.dot(a, wab_ref[...], preferred_element_type=F32)
    ub_ref[SUBLANES:SUBLANES + tm, :] = jnp.dot(a, wbb_ref[...], preferred_element_type=F32)

    def conv(u_ref, cw_ref, cb_ref):
        u = u_ref[...]
        y = cb_ref[...] + cw_ref[taps - 1:taps, :] * u[SUBLANES:, :]
        for back in range(1, taps):
            j = taps - 1 - back
            y = y + cw_ref[j:j + 1, :] * pltpu.roll(u, back, 0)[SUBLANES:, :]
        return y

    ya = conv(ua_ref, cwa_ref, cba_ref)
    yb = conv(ub_ref, cwb_ref, cbb_ref)
    o_ref[...] = (_silu(ya) * yb).astype(o_ref.dtype)

    keep = taps - 1
    sta_ref[...] = ua_ref[SUBLANES + tm - keep:SUBLANES + tm, :]
    stb_ref[...] = ub_ref[SUBLANES + tm - keep:SUBLANES + tm, :]
    ua_ref[0:SUBLANES, :] = ua_ref[tm:tm + SUBLANES, :]
    ub_ref[0:SUBLANES, :] = ub_ref[tm:tm + SUBLANES, :]


def _mm_ffn(a, w_up, cw, cb, layer, seq_len):
    m, k = a.shape
    f = w_up.shape[2] // 2
    taps = cw.shape[1]
    assert taps - 1 <= SUBLANES
    tm = _pick(seq_len, MM_ROWS, 16)
    tn = _pick(f, MM_COLS // 2, LANES)
    nbh = f // tn
    tiles_per_seq = seq_len // tm
    nseq = m // seq_len
    depth = cb.shape[0]
    cb3 = cb.reshape(depth, 1, 2 * f)
    kern = functools.partial(_mm_ffn_kernel, tm=tm, taps=taps, tiles_per_seq=tiles_per_seq)
    return pl.pallas_call(
        kern,
        grid=(f // tn, m // tm),
        in_specs=[pl.BlockSpec((tm, k), lambda n, i: (i, 0)),
                  pl.BlockSpec((None, k, tn), lambda n, i: (layer, 0, n)),
                  pl.BlockSpec((None, k, tn), lambda n, i: (layer, 0, n + nbh)),
                  pl.BlockSpec((None, taps, tn), lambda n, i: (layer, 0, n)),
                  pl.BlockSpec((None, taps, tn), lambda n, i: (layer, 0, n + nbh)),
                  pl.BlockSpec((None, 1, tn), lambda n, i: (layer, 0, n)),
                  pl.BlockSpec((None, 1, tn), lambda n, i: (layer, 0, n + nbh))],
        out_specs=[pl.BlockSpec((tm, tn), lambda n, i: (i, n)),
                   pl.BlockSpec((None, taps - 1, tn), lambda n, i: (i // tiles_per_seq, 0, n)),
                   pl.BlockSpec((None, taps - 1, tn), lambda n, i: (i // tiles_per_seq, 0, n))],
        out_shape=[jax.ShapeDtypeStruct((m, f), BF16),
                   jax.ShapeDtypeStruct((nseq, taps - 1, f), F32),
                   jax.ShapeDtypeStruct((nseq, taps - 1, f), F32)],
        scratch_shapes=[pltpu.VMEM((k, tn), BF16), pltpu.VMEM((k, tn), BF16),
                        pltpu.VMEM((SUBLANES + tm, tn), F32),
                        pltpu.VMEM((SUBLANES + tm, tn), F32)],
        compiler_params=_params(2),
        name="mm_ffn",
    )(a, w_up, w_up, cw, cw, cb3, cb3)


def _ffn_small_kernel(xa_ref, xb_ref, cwa_ref, cwb_ref, cba_ref, cbb_ref, o_ref, *, t, taps):
    def conv(x_ref, cw_ref, cb_ref):
        y = cb_ref[...][None]
        for j in range(taps):
            y = y + cw_ref[j:j + 1, :][None] * x_ref[:, j:j + t, :]
        return y

    ya = conv(xa_ref, cwa_ref, cba_ref)
    yb = conv(xb_ref, cwb_ref, cbb_ref)
    o_ref[...] = (_silu(ya) * yb).astype(o_ref.dtype)


def _ffn_small(xp, cw, cb, layer):
    b, rows, f2 = xp.shape
    f = f2 // 2
    taps = cw.shape[1]
    t = rows - (taps - 1)
    tn = _pick(f, FFN_SMALL_COLS, LANES)
    nbh = f // tn
    depth = cb.shape[0]
    cb3 = cb.reshape(depth, 1, f2)
    kern = functools.partial(_ffn_small_kernel, t=t, taps=taps)
    return pl.pallas_call(
        kern,
        grid=(f // tn,),
        in_specs=[pl.BlockSpec((b, rows, tn), lambda n: (0, 0, n)),
                  pl.BlockSpec((b, rows, tn), lambda n: (0, 0, n + nbh)),
                  pl.BlockSpec((None, taps, tn), lambda n: (layer, 0, n)),
                  pl.BlockSpec((None, taps, tn), lambda n: (layer, 0, n + nbh)),
                  pl.BlockSpec((None, 1, tn), lambda n: (layer, 0, n)),
                  pl.BlockSpec((None, 1, tn), lambda n: (layer, 0, n + nbh))],
        out_specs=pl.BlockSpec((b, t, tn), lambda n: (0, 0, n)),
        out_shape=jax.ShapeDtypeStruct((b, t, f), BF16),
        compiler_params=_params(1),
        name="ffn_small",
    )(xp, xp, cw, cw, cb3, cb3)


CONV_ROWS = 32
CONV_LANES = 256


def _ln_silu(y, g, b):
    mu = jnp.mean(y, axis=-1, keepdims=True)
    yc = y - mu
    var = jnp.mean(yc * yc, axis=-1, keepdims=True)
    return _silu(yc * lax.rsqrt(var + EPS) * g + b)


def _conv_ln_kernel(x_ref, w_ref, b_ref, g_ref, be_ref, o_ref, buf_ref, y_ref,
                    *, tt, taps, halo):
    d = x_ref.shape[1]

    @pl.when(pl.program_id(1) == 0)
    def _():
        buf_ref[0:halo, :] = jnp.zeros((halo, d), F32)

    buf_ref[halo:halo + tt, :] = x_ref[...]
    base = halo - (taps - 1)
    lane_chunk = min(CONV_LANES, d)
    win_rows = halo + CONV_ROWS

    def rows_step(r, carry):
        r0 = pl.multiple_of(r * CONV_ROWS, CONV_ROWS)
        for c0 in range(0, d, lane_chunk):
            win = buf_ref[pl.ds(r0, win_rows), c0:c0 + lane_chunk]
            bias = jnp.broadcast_to(b_ref[:, c0:c0 + lane_chunk], (SUBLANES, lane_chunk))
            accs = [bias] * (CONV_ROWS // SUBLANES)
            for s in range(SUBLANES):
                group = [j for j in range(taps) if (base + j) % SUBLANES == s]
                if not group:
                    continue
                shifted = win if s == 0 else pltpu.roll(win, win_rows - s, 0)
                for j in group:
                    q0 = base + j - s
                    w8 = w_ref[j, :, c0:c0 + lane_chunk]
                    accs = [acc + w8 * shifted[q0 + SUBLANES * i:q0 + SUBLANES * (i + 1), :]
                            for i, acc in enumerate(accs)]
            for i, acc in enumerate(accs):
                y_ref[SUBLANES * i:SUBLANES * (i + 1), c0:c0 + lane_chunk] = acc
        o_ref[pl.ds(r0, CONV_ROWS), :] = _ln_silu(
            y_ref[...], g_ref[...], be_ref[...]).astype(o_ref.dtype)
        return carry

    lax.fori_loop(0, tt // CONV_ROWS, rows_step, 0)
    buf_ref[0:halo, :] = buf_ref[tt:tt + halo, :]


def _conv_ln(x, w_dw, b_dw, ln_g, ln_b, layer):
    b, t, d = x.shape
    nl, taps, _ = w_dw.shape
    halo = -(-(taps - 1) // SUBLANES) * SUBLANES
    tt = _pick(t, CONV_BLOCK_ROWS, CONV_ROWS)
    assert tt % CONV_ROWS == 0 and tt >= halo
    kern = functools.partial(_conv_ln_kernel, tt=tt, taps=taps, halo=halo)
    vec = lambda a: a.reshape(nl, 1, d)
    vspec = pl.BlockSpec((None, 1, d), lambda i, j: (layer, 0, 0))
    w_rep = jnp.broadcast_to(w_dw[:, :, None, :], (nl, taps, SUBLANES, d))
    return pl.pallas_call(
        kern,
        grid=(b, t // tt),
        in_specs=[pl.BlockSpec((None, tt, d), lambda i, j: (i, j, 0)),
                  pl.BlockSpec((None, taps, SUBLANES, d), lambda i, j: (layer, 0, 0, 0)),
                  vspec, vspec, vspec],
        out_specs=pl.BlockSpec((None, tt, d), lambda i, j: (i, j, 0)),
        out_shape=jax.ShapeDtypeStruct((b, t, d), BF16),
        scratch_shapes=[pltpu.VMEM((halo + tt, d), F32),
                        pltpu.VMEM((CONV_ROWS, d), F32)],
        compiler_params=_params(2),
        name="conv_ln",
    )(x, w_rep, vec(b_dw), vec(ln_g), vec(ln_b))


def _conv_small_kernel(xp_ref, w_ref, b_ref, g_ref, be_ref, o_ref, *, t, taps):
    acc = jnp.broadcast_to(b_ref[...], (t, xp_ref.shape[1]))
    for j in range(taps):
        acc = acc + w_ref[j:j + 1, :] * xp_ref[j:j + t, :]
    o_ref[...] = _ln_silu(acc, g_ref[...], be_ref[...]).astype(o_ref.dtype)


def _conv_small(xp, w_dw, b_dw, ln_g, ln_b, layer):
    b, rows, d = xp.shape
    nl, taps, _ = w_dw.shape
    t = rows - (taps - 1)
    kern = functools.partial(_conv_small_kernel, t=t, taps=taps)
    vec = lambda a: a.reshape(nl, 1, d)
    vspec = pl.BlockSpec((None, 1, d), lambda i: (layer, 0, 0))
    return pl.pallas_call(
        kern,
        grid=(b,),
        in_specs=[pl.BlockSpec((None, rows, d), lambda i: (i, 0, 0)),
                  pl.BlockSpec((None, taps, d), lambda i: (layer, 0, 0)),
                  vspec, vspec, vspec],
        out_specs=pl.BlockSpec((None, t, d), lambda i: (i, 0, 0)),
        out_shape=jax.ShapeDtypeStruct((b, t, d), BF16),
        compiler_params=_params(1),
        name="conv_small",
    )(xp, w_dw, vec(b_dw), vec(ln_g), vec(ln_b))


def _log_sigmoid(z):
    return jnp.minimum(z, 0.0) - jnp.log1p(jnp.exp(-jnp.abs(z)))


def _logf_kernel(a_ref, w_ref, b_ref, lf_ref, nf_ref, carry_ref, *, tiles_per_seq):
    @pl.when(pl.program_id(0) % tiles_per_seq == 0)
    def _():
        carry_ref[...] = jnp.zeros(carry_ref.shape, F32)

    z = jnp.dot(a_ref[...], w_ref[...].astype(BF16), preferred_element_type=F32) + b_ref[...]
    lf = _log_sigmoid(z)
    lf_ref[...] = lf
    c = _cumsum_rows(lf) + carry_ref[...]
    nf_ref[...] = -c
    tm = lf.shape[0]
    carry_ref[...] = c[tm - 1:tm, :]


def _logf(a, w_f, b_f, seq_len):
    m, k = a.shape
    tm = _pick(seq_len, LOGF_ROWS, 16)
    kern = functools.partial(_logf_kernel, tiles_per_seq=seq_len // tm)
    return pl.pallas_call(
        kern,
        grid=(m // tm,),
        in_specs=[pl.BlockSpec((tm, k), lambda i: (i, 0)),
                  pl.BlockSpec((k, LANES), lambda i: (0, 0)),
                  pl.BlockSpec((1, LANES), lambda i: (0, 0))],
        out_specs=[pl.BlockSpec((tm, LANES), lambda i: (i, 0)),
                   pl.BlockSpec((tm, LANES), lambda i: (i, 0))],
        out_shape=[jax.ShapeDtypeStruct((m, LANES), F32),
                   jax.ShapeDtypeStruct((m, LANES), F32)],
        scratch_shapes=[pltpu.VMEM((1, LANES), F32)],
        compiler_params=_params(1),
        name="logf",
    )(a, w_f, b_f)


def _flash_step(s, v, m_ref, l_ref, acc_ref):
    m_prev = m_ref[...]
    m_new = jnp.maximum(m_prev, jnp.max(s, axis=1, keepdims=True))
    alpha = jnp.exp(m_prev - m_new)
    p = jnp.exp(s - m_new)
    l_ref[...] = alpha * l_ref[...] + jnp.sum(p, axis=1, keepdims=True)
    acc_ref[...] = alpha * acc_ref[...] + jnp.dot(
        p.astype(BF16), v, preferred_element_type=F32)
    m_ref[...] = m_new


def _attn_kernel(q_ref, k_ref, v_ref, nf_ref, o_ref, kt_ref, va_ref, m_ref, acc_ref,
                 *, tq, dh, heads):
    qi = pl.program_id(2)
    t = k_ref.shape[0]

    @pl.when(qi == 0)
    def _():
        for h in range(heads):
            lanes = slice(h * dh, (h + 1) * dh)
            for c0 in range(0, t, tq):
                kt_ref[h, :, c0:c0 + tq] = k_ref[c0:c0 + tq, lanes].astype(F32).T.astype(BF16)
            va_ref[h, :, 0:dh] = v_ref[:, lanes]
            va_ref[h, :, dh:2 * dh] = jnp.ones((t, dh), BF16)

    m_ref[...] = jnp.full(m_ref.shape, NEG_INF, F32)
    acc_ref[...] = jnp.zeros(acc_ref.shape, F32)

    def chunk(c, diagonal):
        k0 = pl.multiple_of(c * tq, tq)
        for h in range(heads):
            s = jnp.dot(q_ref[:, h * dh:(h + 1) * dh], kt_ref[h, :, pl.ds(k0, tq)],
                        preferred_element_type=F32)
            s = s + nf_ref[h:h + 1, pl.ds(k0, tq)]
            if diagonal:
                row = lax.broadcasted_iota(jnp.int32, s.shape, 0)
                col = lax.broadcasted_iota(jnp.int32, s.shape, 1)
                s = jnp.where(col <= row, s, NEG_INF)
            m_prev = m_ref[h]
            m_new = jnp.maximum(m_prev, jnp.max(s, axis=1, keepdims=True))
            p = jnp.exp(s - m_new).astype(BF16)
            acc_ref[h] = jnp.exp(m_prev - m_new) * acc_ref[h] + jnp.dot(
                p, va_ref[h, pl.ds(k0, tq), :], preferred_element_type=F32)
            m_ref[h] = m_new

    def full_chunk(c, carry):
        chunk(c, False)
        return carry

    lax.fori_loop(0, qi, full_chunk, 0)
    chunk(qi, True)
    for h in range(heads):
        acc = acc_ref[h]
        o_ref[:, h * dh:(h + 1) * dh] = (acc[:, 0:dh] / acc[:, dh:2 * dh]).astype(o_ref.dtype)


def _attention(q, k, v, negf_rows, n_heads):
    b, t, d = q.shape
    dh = d // n_heads
    heads = ATTN_HEADS if n_heads % ATTN_HEADS == 0 else 1
    hw = heads * dh
    tq = _pick(t, ATTN_ROWS, LANES)
    kern = functools.partial(_attn_kernel, tq=tq, dh=dh, heads=heads)
    qspec = pl.BlockSpec((None, tq, hw), lambda bi, hp, qi: (bi, qi, hp))
    kspec = pl.BlockSpec((None, t, hw), lambda bi, hp, qi: (bi, 0, hp))
    return pl.pallas_call(
        kern,
        grid=(b, n_heads // heads, t // tq),
        in_specs=[qspec, kspec, kspec,
                  pl.BlockSpec((None, None, heads, t), lambda bi, hp, qi: (bi, hp, 0, 0))],
        out_specs=qspec,
        out_shape=jax.ShapeDtypeStruct((b, t, d), BF16),
        scratch_shapes=[pltpu.VMEM((heads, dh, t), BF16),
                        pltpu.VMEM((heads, t, 2 * dh), BF16),
                        pltpu.VMEM((heads, tq, 1), F32),
                        pltpu.VMEM((heads, tq, 2 * dh), F32)],
        compiler_params=_params(3),
        name="attention",
    )(q, k, v, negf_rows.reshape(b, n_heads // heads, heads, t))


def _page_cumsum_kernel(pt_ref, *refs, group, n_groups):
    past_refs, new_ref, o_ref, carry_ref = refs[:group], refs[group], refs[group + 1], refs[group + 2]
    p = pl.program_id(1)

    @pl.when(p == 0)
    def _():
        carry_ref[...] = jnp.zeros(carry_ref.shape, F32)

    def run(src_ref, slot):
        c = _cumsum_rows(src_ref[...]) + carry_ref[...]
        o_ref[slot] = -c
        rows = c.shape[0]
        carry_ref[...] = c[rows - 1:rows, :]

    @pl.when(p < n_groups)
    def _():
        for g in range(group):
            run(past_refs[g], g)

    @pl.when(p == n_groups)
    def _():
        run(new_ref, 0)
        for g in range(1, group):
            o_ref[g] = jnp.zeros(o_ref.shape[1:], F32)


def _page_cumsum(page_table, cache_logf, logf_new_pad, layer):
    nb, n_pages = page_table.shape
    _, _, page, h = cache_logf.shape
    group = _pick(n_pages, CUMSUM_PAGES, 1)
    n_groups = n_pages // group
    kern = functools.partial(_page_cumsum_kernel, group=group, n_groups=n_groups)

    def past_spec(g):
        return pl.BlockSpec(
            (None, None, page, h),
            lambda bi, p, pt: (layer, pt[bi * n_pages + jnp.minimum(p, n_groups - 1) * group + g], 0, 0))

    grid_spec = pltpu.PrefetchScalarGridSpec(
        num_scalar_prefetch=1,
        grid=(nb, n_groups + 1),
        in_specs=[past_spec(g) for g in range(group)]
        + [pl.BlockSpec((None, page, h), lambda bi, p, pt: (bi, 0, 0))],
        out_specs=pl.BlockSpec((None, group, page, h), lambda bi, p, pt: (bi, p, 0, 0)),
        scratch_shapes=[pltpu.VMEM((1, h), F32)])
    out = pl.pallas_call(
        kern,
        grid_spec=grid_spec,
        out_shape=jax.ShapeDtypeStruct((nb, (n_groups + 1) * group, page, h), F32),
        compiler_params=_params(2),
        name="page_cumsum",
    )(page_table.reshape(-1), *([cache_logf] * group), logf_new_pad)
    return out[:, :n_pages + 1]


def _decode_kernel(pt_ref, q_ref, *refs, group, n_groups, n_heads):
    kp_refs, vp_refs = refs[:group], refs[group:2 * group]
    kn_ref, vn_ref = refs[2 * group], refs[2 * group + 1]
    nf_refs = refs[2 * group + 2:3 * group + 2]
    o_ref, m_ref, l_ref, acc_ref = refs[3 * group + 2:]
    p = pl.program_id(1)

    @pl.when(p == 0)
    def _():
        m_ref[...] = jnp.full(m_ref.shape, NEG_INF, F32)
        l_ref[...] = jnp.zeros(l_ref.shape, F32)
        acc_ref[...] = jnp.zeros(acc_ref.shape, F32)

    def run(k_ref, v_ref, nf_ref, causal):
        s = lax.dot_general(q_ref[...], k_ref[...].astype(BF16),
                            (((1,), (1,)), ((), ())), preferred_element_type=F32)
        s = s + nf_ref[...]
        row = lax.broadcasted_iota(jnp.int32, s.shape, 0)
        col = lax.broadcasted_iota(jnp.int32, s.shape, 1)
        ok = (row % n_heads) == (col % n_heads)
        if causal:
            ok = ok & ((col // n_heads) <= (row // n_heads))
        s = jnp.where(ok, s, NEG_INF)
        _flash_step(s, v_ref[...].astype(BF16), m_ref, l_ref, acc_ref)

    @pl.when(p < n_groups)
    def _():
        for g in range(group):
            run(kp_refs[g], vp_refs[g], nf_refs[g], False)

    @pl.when(p == n_groups)
    def _():
        run(kn_ref, vn_ref, nf_refs[0], True)
        o_ref[...] = acc_ref[...] / l_ref[...]


def _decode_attention(page_table, q_rows, cache_k, cache_v, k_new, v_new, negf, layer, n_heads):
    nb, n_pages = page_table.shape
    _, qr, dh = q_rows.shape
    pr = cache_k.shape[2]
    group = _pick(n_pages, DECODE_PAGES, 1)
    n_groups = n_pages // group
    kern = functools.partial(_decode_kernel, group=group, n_groups=n_groups, n_heads=n_heads)

    def page_spec(g):
        return pl.BlockSpec(
            (None, None, pr, dh),
            lambda bi, p, pt: (layer, pt[bi * n_pages + jnp.minimum(p, n_groups - 1) * group + g], 0, 0))

    def bias_spec(g):
        return pl.BlockSpec((None, None, 1, pr),
                            lambda bi, p, pt: (bi, jnp.minimum(p * group + g, n_pages), 0, 0))

    new_spec = pl.BlockSpec((None, pr, dh), lambda bi, p, pt: (bi, 0, 0))
    pages = [page_spec(g) for g in range(group)]
    grid_spec = pltpu.PrefetchScalarGridSpec(
        num_scalar_prefetch=1,
        grid=(nb, n_groups + 1),
        in_specs=[pl.BlockSpec((None, qr, dh), lambda bi, p, pt: (bi, 0, 0))]
        + pages + pages + [new_spec, new_spec] + [bias_spec(g) for g in range(group)],
        out_specs=pl.BlockSpec((None, qr, dh), lambda bi, p, pt: (bi, 0, 0)),
        scratch_shapes=[pltpu.VMEM((qr, 1), F32), pltpu.VMEM((qr, 1), F32),
                        pltpu.VMEM((qr, dh), F32)])
    return pl.pallas_call(
        kern,
        grid_spec=grid_spec,
        out_shape=jax.ShapeDtypeStruct((nb, qr, dh), F32),
        compiler_params=_params(2),
        name="decode_attention",
    )(page_table.reshape(-1), q_rows, *([cache_k] * group), *([cache_v] * group),
      k_new, v_new, *([negf] * group))


def _forget_weights(w_qkvf, b_f, layer, d, n_heads):
    w_f = jnp.pad(w_qkvf[layer, :, 3 * d:], ((0, 0), (0, LANES - n_heads)))
    bias = jnp.pad(b_f[layer], (0, LANES - n_heads)).reshape(1, LANES)
    return w_f, bias


def _prompt_trunk(x, mod, p):
    b, t, d = x.shape
    m = b * t
    depth = p['w_up'].shape[0]
    n_heads = p['b_f'].shape[1]
    x2 = x.reshape(m, d)
    k_stack = v_stack = None
    new_logf, new_conv, new_ffn = [], [], []
    for i in range(depth):
        sh_m, sc_m, gt_m, sh_f, sc_f, gt_f = jnp.split(mod[i], 6, axis=-1)
        h = _norm_mod(x2.reshape(b, t, d), p['g_mix'][i], sh_m, sc_m).reshape(m, d)
        j = i // 2
        if i % 2 == 0:
            glu = _mm_glu(h, p['w_pw1'], j).reshape(b, t, d)
            taps = p['w_dw'].shape[1]
            new_conv.append(glu[:, t - (taps - 1):])
            y = _conv_ln(glu, p['w_dw'], p['b_dw'], p['ln_g'], p['ln_b'], j).reshape(m, d)
            x2 = _mm_res(y, p['w_pw2'], j, x2, gt_m.reshape(b, 1, d), t)
        else:
            scale = (d // n_heads) ** -0.5
            wt = p['w_qkvf_t']
            q_bf, = _mm_proj(h, wt, j, 0, d, want_f32=False, want_bf16=True, bf16_scale=scale)
            k_stack, k_bf = _mm_proj(h, wt, j, d, d, prev=k_stack, want_bf16=True)
            v_stack, v_bf = _mm_proj(h, wt, j, 2 * d, d, prev=v_stack, want_bf16=True)
            w_f, bias_f = _forget_weights(p['w_qkvf'], p['b_f'], j, d, n_heads)
            logf, negf = _logf(h, w_f, bias_f, t)
            negf_rows = jnp.transpose(negf.reshape(b, t, LANES)[:, :, :n_heads], (0, 2, 1))
            o = _attention(q_bf.reshape(b, t, d), k_bf.reshape(b, t, d), v_bf.reshape(b, t, d),
                           negf_rows, n_heads).reshape(m, d)
            x2 = _mm_res(o, p['w_o'], j, x2, gt_m.reshape(b, 1, d), t)
            new_logf.append(logf.reshape(b, t, LANES)[:, :, :n_heads])
        h = _norm_mod(x2.reshape(b, t, d), p['g_ffn'][i], sh_f, sc_f).reshape(m, d)
        act, st_a, st_b = _mm_ffn(h, p['w_up'], p['w_ffn_dw'], p['b_ffn_dw'], i, t)
        new_ffn.append(jnp.concatenate([st_a, st_b], axis=-1))
        x2 = _mm_res_bf16w(act, p['w_down_bf'], i, x2, gt_f.reshape(b, 1, d), t)
    y = _final_norm(x2.reshape(b, t, d), p['g_out'])
    heads_shape = (-1, b, t, n_heads, d // n_heads)
    return (y, k_stack.reshape(heads_shape), v_stack.reshape(heads_shape), jnp.stack(new_logf),
            jnp.stack(new_conv), jnp.stack(new_ffn))


def _sample_trunk(x, mod, p, cache_k, cache_v, cache_logf, state_conv, state_ffn, page_table):
    b, t, d = x.shape
    m = b * t
    depth = p['w_up'].shape[0]
    n_heads = p['b_f'].shape[1]
    dh = d // n_heads
    nl, n_pool, page, _, _ = cache_k.shape
    ck = cache_k.reshape(nl, n_pool, page * n_heads, dh)
    cv = cache_v.reshape(nl, n_pool, page * n_heads, dh)
    x2 = x.reshape(m, d)
    rep = lambda g: jnp.repeat(g, t, axis=0).reshape(1, m, d)
    pad_page = lambda a: jnp.pad(a, ((0, 0), (0, page - t)) + ((0, 0),) * (a.ndim - 2))
    new_k, new_v, new_logf, new_conv, new_ffn = [], [], [], [], []
    for i in range(depth):
        sh_m, sc_m, gt_m, sh_f, sc_f, gt_f = jnp.split(mod[i], 6, axis=-1)
        h = _norm_mod(x2.reshape(b, t, d), p['g_mix'][i], sh_m, sc_m).reshape(m, d)
        j = i // 2
        if i % 2 == 0:
            glu = _mm_glu(h, p['w_pw1'], j).reshape(b, t, d)
            xp = jnp.concatenate([state_conv[j], glu], axis=1)
            taps = p['w_dw'].shape[1]
            new_conv.append(xp[:, xp.shape[1] - (taps - 1):])
            y = _conv_small(xp, p['w_dw'], p['b_dw'], p['ln_g'], p['ln_b'], j).reshape(m, d)
            x2 = _mm_res(y, p['w_pw2'], j, x2, rep(gt_m), m)
        else:
            wt = p['w_qkvf_t']
            q, = _mm_proj(h, wt, j, 0, d, want_f32=False, want_bf16=True, bf16_scale=dh ** -0.5)
            k = _mm_proj(h, wt, j, d, d)[0][0]
            v = _mm_proj(h, wt, j, 2 * d, d)[0][0]
            w_f, bias_f = _forget_weights(p['w_qkvf'], p['b_f'], j, d, n_heads)
            logf = _logf(h, w_f, bias_f, m)[0].reshape(b, t, LANES)[:, :, :n_heads]
            negf = _page_cumsum(page_table, cache_logf, pad_page(logf), j)
            k_new = pad_page(k.reshape(b, t, n_heads, dh)).reshape(b, page * n_heads, dh)
            v_new = pad_page(v.reshape(b, t, n_heads, dh)).reshape(b, page * n_heads, dh)
            o = _decode_attention(
                page_table, q.reshape(b, t * n_heads, dh), ck, cv, k_new, v_new,
                negf.reshape(b, negf.shape[1], 1, page * n_heads), j, n_heads)
            x2 = _mm_res(o.reshape(m, d).astype(BF16), p['w_o'], j, x2, rep(gt_m), m)
            new_k.append(k.reshape(b, t, n_heads, dh))
            new_v.append(v.reshape(b, t, n_heads, dh))
            new_logf.append(logf)
        h = _norm_mod(x2.reshape(b, t, d), p['g_ffn'][i], sh_f, sc_f).reshape(m, d)
        f2 = p['w_up'].shape[2]
        u = _mm_plain(h, p['w_up'], i, 0, f2)[0].reshape(b, t, f2)
        xp = jnp.concatenate([state_ffn[i], u], axis=1)
        ftaps = p['w_ffn_dw'].shape[1]
        new_ffn.append(xp[:, xp.shape[1] - (ftaps - 1):])
        act = _ffn_small(xp, p['w_ffn_dw'], p['b_ffn_dw'], i).reshape(m, f2 // 2)
        x2 = _mm_res_bf16w(act, p['w_down_bf'], i, x2, rep(gt_f), m)
    y = _final_norm(x2.reshape(b, t, d), p['g_out'])
    return (y, jnp.stack(new_k), jnp.stack(new_v), jnp.stack(new_logf),
            jnp.stack(new_conv), jnp.stack(new_ffn))


def kernel(x_prompt, x_sample, c_prompt, c_sample, cache_k, cache_v, cache_logf, state_conv, state_ffn, page_table, w_ada, b_ada, g_mix, g_ffn, g_out, w_pw1, w_dw, b_dw, ln_g, ln_b, w_pw2, w_qkvf, b_f, w_o, w_up, w_ffn_dw, b_ffn_dw, w_down):
    p = dict(g_mix=g_mix, g_ffn=g_ffn, g_out=g_out, w_pw1=w_pw1, w_dw=w_dw, b_dw=b_dw,
             ln_g=ln_g, ln_b=ln_b, w_pw2=w_pw2, w_qkvf=w_qkvf,
             w_qkvf_t=jnp.swapaxes(w_qkvf, 1, 2), b_f=b_f, w_o=w_o,
             w_up=w_up, w_ffn_dw=w_ffn_dw, b_ffn_dw=b_ffn_dw, w_down_bf=_cast_bf16(w_down))
    nb_p = c_prompt.shape[0]
    nb_s = c_sample.shape[0]
    c_all = jnp.concatenate([c_prompt, c_sample], axis=0)
    pad = (-c_all.shape[0]) % 16
    c_all = jnp.pad(c_all, ((0, pad), (0, 0)))
    mod = _ada(c_all, w_ada, b_ada)
    outs_p = _prompt_trunk(x_prompt, mod[:, :nb_p], p)
    outs_s = _sample_trunk(x_sample, mod[:, nb_p:nb_p + nb_s], p, cache_k, cache_v,
                           cache_logf, state_conv, state_ffn, page_table)
    return (outs_p[0], outs_s[0]) + outs_p[1:] + outs_s[1:]
```

```python
import functools

import jax
import jax.numpy as jnp
from jax import lax
from jax.experimental import pallas as pl
from jax.experimental.pallas import tpu as pltpu

F32 = jnp.float32
BF16 = jnp.bfloat16
EPS = 1e-6
NEG_INF = -1e30

V7X_VMEM_BYTES = 64 * 1024 * 1024
VMEM_LIMIT = V7X_VMEM_BYTES - 8 * 1024 * 1024
LANES = 128
SUBLANES = 8

MM_ROWS = 1024
MM_ROWS_BIG_K = 512
MM_COLS = 512
MM_COLS_BIG_K = 256
MM_SMALL_K = 4096
NORM_ROWS = 256
CONV_BLOCK_ROWS = 256
LOGF_ROWS = 512
ATTN_ROWS = 512
ATTN_HEADS = 4
FFN_SMALL_COLS = 5504
ADA_COLS = 512
CAST_ROWS = 688
CUMSUM_PAGES = 8
DECODE_PAGES = 4


def _pick(dim, pref, align):
    if dim <= pref:
        return dim
    t = (pref // align) * align
    while t >= align:
        if dim % t == 0:
            return t
        t -= align
    return dim


def _params(n_axes):
    return pltpu.CompilerParams(
        dimension_semantics=("arbitrary",) * n_axes,
        vmem_limit_bytes=VMEM_LIMIT)


def _silu(x):
    return x * jax.nn.sigmoid(x)


def _cumsum_rows(x):
    rows = x.shape[0]
    idx = lax.broadcasted_iota(jnp.int32, x.shape, 0)
    s = 1
    while s < rows:
        x = x + jnp.where(idx >= s, pltpu.roll(x, s, 0), 0.0)
        s *= 2
    return x


def _ada_kernel(c_ref, w_ref, b_ref, o_ref):
    cs = _silu(c_ref[...]).astype(BF16)
    o_ref[...] = jnp.dot(cs, w_ref[...].astype(BF16),
                         preferred_element_type=F32) + b_ref[...]


def _ada(c, w_ada, b_ada):
    depth, d, n = w_ada.shape
    rows = c.shape[0]
    tn = _pick(n, ADA_COLS, LANES)
    return pl.pallas_call(
        _ada_kernel,
        grid=(depth, n // tn),
        in_specs=[pl.BlockSpec((rows, d), lambda i, j: (0, 0)),
                  pl.BlockSpec((None, d, tn), lambda i, j: (i, 0, j)),
                  pl.BlockSpec((None, 1, tn), lambda i, j: (i, 0, j))],
        out_specs=pl.BlockSpec((None, rows, tn), lambda i, j: (i, 0, j)),
        out_shape=jax.ShapeDtypeStruct((depth, rows, n), F32),
        compiler_params=_params(2),
        name="ada",
    )(c, w_ada, b_ada.reshape(depth, 1, n))


def _norm_mod_kernel(x_ref, g_ref, sh_ref, sc_ref, o_ref):
    x = x_ref[...]
    y = x * lax.rsqrt(jnp.mean(x * x, axis=-1, keepdims=True) + EPS) * g_ref[...]
    o_ref[...] = (y * (1.0 + sc_ref[...]) + sh_ref[...]).astype(o_ref.dtype)


def _norm_kernel(x_ref, g_ref, o_ref):
    x = x_ref[...]
    o_ref[...] = x * lax.rsqrt(jnp.mean(x * x, axis=-1, keepdims=True) + EPS) * g_ref[...]


def _norm_mod(x, g, shift, scale):
    b, t, d = x.shape
    tt = _pick(t, NORM_ROWS, 16)
    return pl.pallas_call(
        _norm_mod_kernel,
        grid=(b, t // tt),
        in_specs=[pl.BlockSpec((None, tt, d), lambda i, j: (i, j, 0)),
                  pl.BlockSpec((1, d), lambda i, j: (0, 0)),
                  pl.BlockSpec((None, 1, d), lambda i, j: (i, 0, 0)),
                  pl.BlockSpec((None, 1, d), lambda i, j: (i, 0, 0))],
        out_specs=pl.BlockSpec((None, tt, d), lambda i, j: (i, j, 0)),
        out_shape=jax.ShapeDtypeStruct((b, t, d), BF16),
        compiler_params=_params(2),
        name="norm_mod",
    )(x, g.reshape(1, d), shift.reshape(b, 1, d), scale.reshape(b, 1, d))


def _final_norm(x, g):
    b, t, d = x.shape
    tt = _pick(t, NORM_ROWS, SUBLANES)
    return pl.pallas_call(
        _norm_kernel,
        grid=(b, t // tt),
        in_specs=[pl.BlockSpec((None, tt, d), lambda i, j: (i, j, 0)),
                  pl.BlockSpec((1, d), lambda i, j: (0, 0))],
        out_specs=pl.BlockSpec((None, tt, d), lambda i, j: (i, j, 0)),
        out_shape=jax.ShapeDtypeStruct((b, t, d), F32),
        compiler_params=_params(2),
        name="final_norm",
    )(x, g.reshape(1, d))


def _cast_once(w_ref, wb_ref):
    @pl.when(pl.program_id(1) == 0)
    def _():
        wb_ref[...] = w_ref[...].astype(BF16)


def _mm_plain_kernel(a_ref, w_ref, *rest, bf16_scale):
    out_refs, wb_ref = rest[:-1], rest[-1]
    _cast_once(w_ref, wb_ref)
    acc = jnp.dot(a_ref[...], wb_ref[...], preferred_element_type=F32)
    for o_ref in out_refs:
        if o_ref.dtype == BF16:
            o_ref[...] = (acc * bf16_scale).astype(BF16)
        else:
            o_ref[...] = acc


def _mm_proj_kernel(a_ref, wt_ref, *rest, bf16_scale, n_prev, want_f32, want_bf16):
    rest = list(rest)
    wb_ref = rest.pop()
    prev_ref = rest.pop(0) if n_prev else None
    _cast_once(wt_ref, wb_ref)
    acc = lax.dot_general(a_ref[...], wb_ref[...], (((1,), (1,)), ((), ())),
                          preferred_element_type=F32)
    if want_f32:
        stack_ref = rest.pop(0)
        if n_prev:
            stack_ref[0:n_prev] = prev_ref[...]
        stack_ref[n_prev] = acc
    if want_bf16:
        rest.pop(0)[...] = (acc * bf16_scale).astype(BF16)


def _mm_res_bf16w_kernel(a_ref, w_ref, x_ref, g_ref, o_ref):
    acc = jnp.dot(a_ref[...], w_ref[...], preferred_element_type=F32)
    o_ref[...] = x_ref[...] + g_ref[...] * acc


def _cast_kernel(x_ref, o_ref):
    o_ref[...] = x_ref[...].astype(o_ref.dtype)


def _mm_glu_kernel(a_ref, wa_ref, wg_ref, o_ref, wab_ref, wgb_ref):
    _cast_once(wa_ref, wab_ref)
    _cast_once(wg_ref, wgb_ref)
    a = a_ref[...]
    va = jnp.dot(a, wab_ref[...], preferred_element_type=F32)
    vg = jnp.dot(a, wgb_ref[...], preferred_element_type=F32)
    o_ref[...] = va * jax.nn.sigmoid(vg)


def _mm_res_kernel(a_ref, w_ref, x_ref, g_ref, o_ref, wb_ref):
    _cast_once(w_ref, wb_ref)
    acc = jnp.dot(a_ref[...], wb_ref[...], preferred_element_type=F32)
    o_ref[...] = x_ref[...] + g_ref[...] * acc


def _mm_tiles(m, k):
    tm = _pick(m, MM_ROWS if k <= MM_SMALL_K else MM_ROWS_BIG_K, 16)
    tn_pref = MM_COLS if k <= MM_SMALL_K else MM_COLS_BIG_K
    return tm, tn_pref


def _mm_plain(a, w, layer, col0, ncols, out_dtypes=(F32,), bf16_scale=1.0):
    m, k = a.shape
    tm, tn_pref = _mm_tiles(m, k)
    tn = _pick(ncols, tn_pref, LANES)
    nb0 = col0 // tn
    assert col0 % tn == 0
    kern = functools.partial(_mm_plain_kernel, bf16_scale=bf16_scale)
    return pl.pallas_call(
        kern,
        grid=(ncols // tn, m // tm),
        in_specs=[pl.BlockSpec((tm, k), lambda n, i: (i, 0)),
                  pl.BlockSpec((None, k, tn), lambda n, i: (layer, 0, n + nb0))],
        out_specs=[pl.BlockSpec((tm, tn), lambda n, i: (i, n)) for _ in out_dtypes],
        out_shape=[jax.ShapeDtypeStruct((m, ncols), dt) for dt in out_dtypes],
        scratch_shapes=[pltpu.VMEM((k, tn), BF16)],
        compiler_params=_params(2),
        name="mm_plain",
    )(a, w)


def _mm_proj(a, wt, layer, col0, ncols, prev=None, want_f32=True, want_bf16=False,
             bf16_scale=1.0):
    m, k = a.shape
    tm, tn_pref = _mm_tiles(m, k)
    tn = _pick(ncols, tn_pref, LANES)
    nb0 = col0 // tn
    assert col0 % tn == 0
    n_prev = 0 if prev is None else prev.shape[0]
    kern = functools.partial(_mm_proj_kernel, bf16_scale=bf16_scale, n_prev=n_prev,
                             want_f32=want_f32, want_bf16=want_bf16)
    in_specs = [pl.BlockSpec((tm, k), lambda n, i: (i, 0)),
                pl.BlockSpec((None, tn, k), lambda n, i: (layer, n + nb0, 0))]
    operands = [a, wt]
    if n_prev:
        in_specs.append(pl.BlockSpec((n_prev, tm, tn), lambda n, i: (0, i, n)))
        operands.append(prev)
    out_specs, out_shape = [], []
    if want_f32:
        out_specs.append(pl.BlockSpec((n_prev + 1, tm, tn), lambda n, i: (0, i, n)))
        out_shape.append(jax.ShapeDtypeStruct((n_prev + 1, m, ncols), F32))
    if want_bf16:
        out_specs.append(pl.BlockSpec((tm, tn), lambda n, i: (i, n)))
        out_shape.append(jax.ShapeDtypeStruct((m, ncols), BF16))
    return pl.pallas_call(
        kern,
        grid=(ncols // tn, m // tm),
        in_specs=in_specs,
        out_specs=out_specs,
        out_shape=out_shape,
        scratch_shapes=[pltpu.VMEM((tn, k), BF16)],
        compiler_params=_params(2),
        name="mm_proj",
    )(*operands)


def _cast_bf16(w):
    nl, k, n = w.shape
    tk = _pick(k, CAST_ROWS, 16)
    return pl.pallas_call(
        _cast_kernel,
        grid=(nl, k // tk),
        in_specs=[pl.BlockSpec((None, tk, n), lambda l, i: (l, i, 0))],
        out_specs=pl.BlockSpec((None, tk, n), lambda l, i: (l, i, 0)),
        out_shape=jax.ShapeDtypeStruct((nl, k, n), BF16),
        compiler_params=_params(2),
        name="cast_bf16",
    )(w)


def _mm_res_bf16w(a, w_bf, layer, x, gate, rows_per_gate):
    m, k = a.shape
    n_out = w_bf.shape[2]
    tm = _pick(m, MM_ROWS_BIG_K, 16)
    tn = _pick(n_out, MM_COLS, LANES)
    r = gate.shape[1]
    assert rows_per_gate % tm == 0 and r in (1, tm)
    return pl.pallas_call(
        _mm_res_bf16w_kernel,
        grid=(n_out // tn, m // tm),
        in_specs=[pl.BlockSpec((tm, k), lambda n, i: (i, 0)),
                  pl.BlockSpec((None, k, tn), lambda n, i: (layer, 0, n)),
                  pl.BlockSpec((tm, tn), lambda n, i: (i, n)),
                  pl.BlockSpec((None, r, tn),
                               lambda n, i: ((i * tm) // rows_per_gate, 0, n))],
        out_specs=pl.BlockSpec((tm, tn), lambda n, i: (i, n)),
        out_shape=jax.ShapeDtypeStruct((m, n_out), F32),
        compiler_params=_params(2),
        name="mm_res_bf16w",
    )(a, w_bf, x, gate)


def _mm_glu(a, w, layer):
    m, k = a.shape
    d = w.shape[2] // 2
    tm, tn_pref = _mm_tiles(m, k)
    tn = _pick(d, tn_pref // 2, LANES)
    nbg = d // tn
    return pl.pallas_call(
        _mm_glu_kernel,
        grid=(d // tn, m // tm),
        in_specs=[pl.BlockSpec((tm, k), lambda n, i: (i, 0)),
                  pl.BlockSpec((None, k, tn), lambda n, i: (layer, 0, n)),
                  pl.BlockSpec((None, k, tn), lambda n, i: (layer, 0, n + nbg))],
        out_specs=pl.BlockSpec((tm, tn), lambda n, i: (i, n)),
        out_shape=jax.ShapeDtypeStruct((m, d), F32),
        scratch_shapes=[pltpu.VMEM((k, tn), BF16), pltpu.VMEM((k, tn), BF16)],
        compiler_params=_params(2),
        name="mm_glu",
    )(a, w, w)


def _mm_res(a, w, layer, x, gate, rows_per_gate):
    m, k = a.shape
    n_out = w.shape[2]
    tm, tn_pref = _mm_tiles(m, k)
    tn = _pick(n_out, tn_pref, LANES)
    r = gate.shape[1]
    assert rows_per_gate % tm == 0 and r in (1, tm)
    return pl.pallas_call(
        _mm_res_kernel,
        grid=(n_out // tn, m // tm),
        in_specs=[pl.BlockSpec((tm, k), lambda n, i: (i, 0)),
                  pl.BlockSpec((None, k, tn), lambda n, i: (layer, 0, n)),
                  pl.BlockSpec((tm, tn), lambda n, i: (i, n)),
                  pl.BlockSpec((None, r, tn),
                               lambda n, i: ((i * tm) // rows_per_gate, 0, n))],
        out_specs=pl.BlockSpec((tm, tn), lambda n, i: (i, n)),
        out_shape=jax.ShapeDtypeStruct((m, n_out), F32),
        scratch_shapes=[pltpu.VMEM((k, tn), BF16)],
        compiler_params=_params(2),
        name="mm_res",
    )(a, w, x, gate)


RING_SLOTS = 3


def _ring_copy(a_hbm, abuf, sem, step, nm, tm):
    slot = step % RING_SLOTS
    row = pl.multiple_of((step % nm) * tm, tm)
    return pltpu.make_async_copy(a_hbm.at[pl.ds(row, tm), :], abuf.at[slot], sem.at[slot])


def _mm_res_ring_kernel(a_hbm, w_ref, x_ref, g_ref, o_ref, wb_ref, abuf, sem, *, nm, tm, steps):
    s = pl.program_id(0) * nm + pl.program_id(1)

    @pl.when(s == 0)
    def _():
        for first in range(min(RING_SLOTS - 1, steps)):
            _ring_copy(a_hbm, abuf, sem, first, nm, tm).start()

    @pl.when(s + (RING_SLOTS - 1) < steps)
    def _():
        _ring_copy(a_hbm, abuf, sem, s + (RING_SLOTS - 1), nm, tm).start()

    _cast_once(w_ref, wb_ref)
    _ring_copy(a_hbm, abuf, sem, s, nm, tm).wait()
    acc = jnp.dot(abuf[s % RING_SLOTS], wb_ref[...], preferred_element_type=F32)
    o_ref[...] = x_ref[...] + g_ref[...] * acc


def _mm_res_ring(a, w, layer, x, gate, rows_per_gate):
    m, k = a.shape
    n_out = w.shape[2]
    tm, tn_pref = _mm_tiles(m, k)
    tn = _pick(n_out, tn_pref, LANES)
    assert rows_per_gate % tm == 0 and gate.shape[1] == 1
    nn, nm = n_out // tn, m // tm
    kern = functools.partial(_mm_res_ring_kernel, nm=nm, tm=tm, steps=nn * nm)
    return pl.pallas_call(
        kern,
        grid=(nn, nm),
        in_specs=[pl.BlockSpec(memory_space=pl.ANY),
                  pl.BlockSpec((None, k, tn), lambda n, i: (layer, 0, n)),
                  pl.BlockSpec((tm, tn), lambda n, i: (i, n)),
                  pl.BlockSpec((None, 1, tn),
                               lambda n, i: ((i * tm) // rows_per_gate, 0, n))],
        out_specs=pl.BlockSpec((tm, tn), lambda n, i: (i, n)),
        out_shape=jax.ShapeDtypeStruct((m, n_out), F32),
        scratch_shapes=[pltpu.VMEM((k, tn), BF16),
                        pltpu.VMEM((RING_SLOTS, tm, k), BF16),
                        pltpu.SemaphoreType.DMA((RING_SLOTS,))],
        compiler_params=_params(2),
        name="mm_res_ring",
    )(a, w, x, gate)


def _mm_ffn_kernel(a_ref, wa_ref, wb_ref, cwa_ref, cwb_ref, cba_ref, cbb_ref,
                   o_ref, sta_ref, stb_ref, wab_ref, wbb_ref, ua_ref, ub_ref,
                   *, tm, taps, tiles_per_seq):
    _cast_once(wa_ref, wab_ref)
    _cast_once(wb_ref, wbb_ref)

    @pl.when(pl.program_id(1) % tiles_per_seq == 0)
    def _():
        ua_ref[0:SUBLANES, :] = jnp.zeros((SUBLANES, ua_ref.shape[1]), F32)
        ub_ref[0:SUBLANES, :] = jnp.zeros((SUBLANES, ub_ref.shape[1]), F32)

    a = a_ref[...]
    ua_ref[SUBLANES:SUBLANES + tm, :] = jnp.dot(a, wab_ref[...], preferred_element_type=F32)
    ub_ref[SUBLANES:SUBLANES + tm, :] = jnp.dot(a, wbb_ref[...], preferred_element_type=F32)

    def conv(u_ref, cw_ref, cb_ref):
        u = u_ref[...]
        y = cb_ref[...] + cw_ref[taps - 1:taps, :] * u[SUBLANES:, :]
        for back in range(1, taps):
            j = taps - 1 - back
            y = y + cw_ref[j:j + 1, :] * pltpu.roll(u, back, 0)[SUBLANES:, :]
        return y

    ya = conv(ua_ref, cwa_ref, cba_ref)
    yb = conv(ub_ref, cwb_ref, cbb_ref)
    o_ref[...] = (_silu(ya) * yb).astype(o_ref.dtype)

    keep = taps - 1
    sta_ref[...] = ua_ref[SUBLANES + tm - keep:SUBLANES + tm, :]
    stb_ref[...] = ub_ref[SUBLANES + tm - keep:SUBLANES + tm, :]
    ua_ref[0:SUBLANES, :] = ua_ref[tm:tm + SUBLANES, :]
    ub_ref[0:SUBLANES, :] = ub_ref[tm:tm + SUBLANES, :]


def _mm_ffn(a, w_up, cw, cb, layer, seq_len):
    m, k = a.shape
    f = w_up.shape[2] // 2
    taps = cw.shape[1]
    assert taps - 1 <= SUBLANES
    tm = _pick(seq_len, MM_ROWS, 16)
    tn = _pick(f, MM_COLS // 2, LANES)
    nbh = f // tn
    tiles_per_seq = seq_len // tm
    nseq = m // seq_len
    depth = cb.shape[0]
    cb3 = cb.reshape(depth, 1, 2 * f)
    kern = functools.partial(_mm_ffn_kernel, tm=tm, taps=taps, tiles_per_seq=tiles_per_seq)
    return pl.pallas_call(
        kern,
        grid=(f // tn, m // tm),
        in_specs=[pl.BlockSpec((tm, k), lambda n, i: (i, 0)),
                  pl.BlockSpec((None, k, tn), lambda n, i: (layer, 0, n)),
                  pl.BlockSpec((None, k, tn), lambda n, i: (layer, 0, n + nbh)),
                  pl.BlockSpec((None, taps, tn), lambda n, i: (layer, 0, n)),
                  pl.BlockSpec((None, taps, tn), lambda n, i: (layer, 0, n + nbh)),
                  pl.BlockSpec((None, 1, tn), lambda n, i: (layer, 0, n)),
                  pl.BlockSpec((None, 1, tn), lambda n, i: (layer, 0, n + nbh))],
        out_specs=[pl.BlockSpec((tm, tn), lambda n, i: (i, n)),
                   pl.BlockSpec((None, taps - 1, tn), lambda n, i: (i // tiles_per_seq, 0, n)),
                   pl.BlockSpec((None, taps - 1, tn), lambda n, i: (i // tiles_per_seq, 0, n))],
        out_shape=[jax.ShapeDtypeStruct((m, f), BF16),
                   jax.ShapeDtypeStruct((nseq, taps - 1, f), F32),
                   jax.ShapeDtypeStruct((nseq, taps - 1, f), F32)],
        scratch_shapes=[pltpu.VMEM((k, tn), BF16), pltpu.VMEM((k, tn), BF16),
                        pltpu.VMEM((SUBLANES + tm, tn), F32),
                        pltpu.VMEM((SUBLANES + tm, tn), F32)],
        compiler_params=_params(2),
        name="mm_ffn",
    )(a, w_up, w_up, cw, cw, cb3, cb3)


def _ffn_small_kernel(xa_ref, xb_ref, cwa_ref, cwb_ref, cba_ref, cbb_ref, o_ref, *, t, taps):
    def conv(x_ref, cw_ref, cb_ref):
        y = cb_ref[...][None]
        for j in range(taps):
            y = y + cw_ref[j:j + 1, :][None] * x_ref[:, j:j + t, :]
        return y

    ya = conv(xa_ref, cwa_ref, cba_ref)
    yb = conv(xb_ref, cwb_ref, cbb_ref)
    o_ref[...] = (_silu(ya) * yb).astype(o_ref.dtype)


def _ffn_small(xp, cw, cb, layer):
    b, rows, f2 = xp.shape
    f = f2 // 2
    taps = cw.shape[1]
    t = rows - (taps - 1)
    tn = _pick(f, FFN_SMALL_COLS, LANES)
    nbh = f // tn
    depth = cb.shape[0]
    cb3 = cb.reshape(depth, 1, f2)
    kern = functools.partial(_ffn_small_kernel, t=t, taps=taps)
    return pl.pallas_call(
        kern,
        grid=(f // tn,),
        in_specs=[pl.BlockSpec((b, rows, tn), lambda n: (0, 0, n)),
                  pl.BlockSpec((b, rows, tn), lambda n: (0, 0, n + nbh)),
                  pl.BlockSpec((None, taps, tn), lambda n: (layer, 0, n)),
                  pl.BlockSpec((None, taps, tn), lambda n: (layer, 0, n + nbh)),
                  pl.BlockSpec((None, 1, tn), lambda n: (layer, 0, n)),
                  pl.BlockSpec((None, 1, tn), lambda n: (layer, 0, n + nbh))],
        out_specs=pl.BlockSpec((b, t, tn), lambda n: (0, 0, n)),
        out_shape=jax.ShapeDtypeStruct((b, t, f), BF16),
        compiler_params=_params(1),
        name="ffn_small",
    )(xp, xp, cw, cw, cb3, cb3)


CONV_ROWS = 32
CONV_LANES = 256


def _ln_silu(y, g, b):
    mu = jnp.mean(y, axis=-1, keepdims=True)
    yc = y - mu
    var = jnp.mean(yc * yc, axis=-1, keepdims=True)
    return _silu(yc * lax.rsqrt(var + EPS) * g + b)


def _conv_ln_kernel(x_ref, w_ref, b_ref, g_ref, be_ref, o_ref, buf_ref, y_ref,
                    *, tt, taps, halo):
    d = x_ref.shape[1]

    @pl.when(pl.program_id(1) == 0)
    def _():
        buf_ref[0:halo, :] = jnp.zeros((halo, d), F32)

    buf_ref[halo:halo + tt, :] = x_ref[...]
    base = halo - (taps - 1)
    lane_chunk = min(CONV_LANES, d)
    win_rows = halo + CONV_ROWS

    def rows_step(r, carry):
        r0 = pl.multiple_of(r * CONV_ROWS, CONV_ROWS)
        for c0 in range(0, d, lane_chunk):
            win = buf_ref[pl.ds(r0, win_rows), c0:c0 + lane_chunk]
            bias = jnp.broadcast_to(b_ref[:, c0:c0 + lane_chunk], (SUBLANES, lane_chunk))
            accs = [bias] * (CONV_ROWS // SUBLANES)
            for s in range(SUBLANES):
                group = [j for j in range(taps) if (base + j) % SUBLANES == s]
                if not group:
                    continue
                shifted = win if s == 0 else pltpu.roll(win, win_rows - s, 0)
                for j in group:
                    q0 = base + j - s
                    w8 = w_ref[j, :, c0:c0 + lane_chunk]
                    accs = [acc + w8 * shifted[q0 + SUBLANES * i:q0 + SUBLANES * (i + 1), :]
                            for i, acc in enumerate(accs)]
            for i, acc in enumerate(accs):
                y_ref[SUBLANES * i:SUBLANES * (i + 1), c0:c0 + lane_chunk] = acc
        o_ref[pl.ds(r0, CONV_ROWS), :] = _ln_silu(
            y_ref[...], g_ref[...], be_ref[...]).astype(o_ref.dtype)
        return carry

    lax.fori_loop(0, tt // CONV_ROWS, rows_step, 0)
    buf_ref[0:halo, :] = buf_ref[tt:tt + halo, :]


def _conv_ln(x, w_dw, b_dw, ln_g, ln_b, layer):
    b, t, d = x.shape
    nl, taps, _ = w_dw.shape
    halo = -(-(taps - 1) // SUBLANES) * SUBLANES
    tt = _pick(t, CONV_BLOCK_ROWS, CONV_ROWS)
    assert tt % CONV_ROWS == 0 and tt >= halo
    kern = functools.partial(_conv_ln_kernel, tt=tt, taps=taps, halo=halo)
    vec = lambda a: a.reshape(nl, 1, d)
    vspec = pl.BlockSpec((None, 1, d), lambda i, j: (layer, 0, 0))
    w_rep = jnp.broadcast_to(w_dw[:, :, None, :], (nl, taps, SUBLANES, d))
    return pl.pallas_call(
        kern,
        grid=(b, t // tt),
        in_specs=[pl.BlockSpec((None, tt, d), lambda i, j: (i, j, 0)),
                  pl.BlockSpec((None, taps, SUBLANES, d), lambda i, j: (layer, 0, 0, 0)),
                  vspec, vspec, vspec],
        out_specs=pl.BlockSpec((None, tt, d), lambda i, j: (i, j, 0)),
        out_shape=jax.ShapeDtypeStruct((b, t, d), BF16),
        scratch_shapes=[pltpu.VMEM((halo + tt, d), F32),
                        pltpu.VMEM((CONV_ROWS, d), F32)],
        compiler_params=_params(2),
        name="conv_ln",
    )(x, w_rep, vec(b_dw), vec(ln_g), vec(ln_b))


def _conv_small_kernel(xp_ref, w_ref, b_ref, g_ref, be_ref, o_ref, *, t, taps):
    acc = jnp.broadcast_to(b_ref[...], (t, xp_ref.shape[1]))
    for j in range(taps):
        acc = acc + w_ref[j:j + 1, :] * xp_ref[j:j + t, :]
    o_ref[...] = _ln_silu(acc, g_ref[...], be_ref[...]).astype(o_ref.dtype)


def _conv_small(xp, w_dw, b_dw, ln_g, ln_b, layer):
    b, rows, d = xp.shape
    nl, taps, _ = w_dw.shape
    t = rows - (taps - 1)
    kern = functools.partial(_conv_small_kernel, t=t, taps=taps)
    vec = lambda a: a.reshape(nl, 1, d)
    vspec = pl.BlockSpec((None, 1, d), lambda i: (layer, 0, 0))
    return pl.pallas_call(
        kern,
        grid=(b,),
        in_specs=[pl.BlockSpec((None, rows, d), lambda i: (i, 0, 0)),
                  pl.BlockSpec((None, taps, d), lambda i: (layer, 0, 0)),
                  vspec, vspec, vspec],
        out_specs=pl.BlockSpec((None, t, d), lambda i: (i, 0, 0)),
        out_shape=jax.ShapeDtypeStruct((b, t, d), BF16),
        compiler_params=_params(1),
        name="conv_small",
    )(xp, w_dw, vec(b_dw), vec(ln_g), vec(ln_b))


def _log_sigmoid(z):
    return jnp.minimum(z, 0.0) - jnp.log1p(jnp.exp(-jnp.abs(z)))


def _logf_kernel(a_ref, w_ref, b_ref, lf_ref, nf_ref, carry_ref, *, tiles_per_seq):
    @pl.when(pl.program_id(0) % tiles_per_seq == 0)
    def _():
        carry_ref[...] = jnp.zeros(carry_ref.shape, F32)

    z = jnp.dot(a_ref[...], w_ref[...].astype(BF16), preferred_element_type=F32) + b_ref[...]
    lf = _log_sigmoid(z)
    lf_ref[...] = lf
    c = _cumsum_rows(lf) + carry_ref[...]
    nf_ref[...] = -c
    tm = lf.shape[0]
    carry_ref[...] = c[tm - 1:tm, :]


def _logf(a, w_f, b_f, seq_len):
    m, k = a.shape
    tm = _pick(seq_len, LOGF_ROWS, 16)
    kern = functools.partial(_logf_kernel, tiles_per_seq=seq_len // tm)
    return pl.pallas_call(
        kern,
        grid=(m // tm,),
        in_specs=[pl.BlockSpec((tm, k), lambda i: (i, 0)),
                  pl.BlockSpec((k, LANES), lambda i: (0, 0)),
                  pl.BlockSpec((1, LANES), lambda i: (0, 0))],
        out_specs=[pl.BlockSpec((tm, LANES), lambda i: (i, 0)),
                   pl.BlockSpec((tm, LANES), lambda i: (i, 0))],
        out_shape=[jax.ShapeDtypeStruct((m, LANES), F32),
                   jax.ShapeDtypeStruct((m, LANES), F32)],
        scratch_shapes=[pltpu.VMEM((1, LANES), F32)],
        compiler_params=_params(1),
        name="logf",
    )(a, w_f, b_f)


def _flash_step(s, v, m_ref, l_ref, acc_ref):
    m_prev = m_ref[...]
    m_new = jnp.maximum(m_prev, jnp.max(s, axis=1, keepdims=True))
    alpha = jnp.exp(m_prev - m_new)
    p = jnp.exp(s - m_new)
    l_ref[...] = alpha * l_ref[...] + jnp.sum(p, axis=1, keepdims=True)
    acc_ref[...] = alpha * acc_ref[...] + jnp.dot(
        p.astype(BF16), v, preferred_element_type=F32)
    m_ref[...] = m_new


def _attn_kernel(q_ref, k_ref, v_ref, nf_ref, o_ref, kt_ref, va_ref, m_ref, acc_ref,
                 *, tq, dh, heads):
    qi = pl.program_id(2)
    t = k_ref.shape[0]

    @pl.when(qi == 0)
    def _():
        for h in range(heads):
            lanes = slice(h * dh, (h + 1) * dh)
            for c0 in range(0, t, tq):
                kt_ref[h, :, c0:c0 + tq] = k_ref[c0:c0 + tq, lanes].astype(F32).T.astype(BF16)
            va_ref[h, :, 0:dh] = v_ref[:, lanes]
            va_ref[h, :, dh:2 * dh] = jnp.ones((t, dh), BF16)

    m_ref[...] = jnp.full(m_ref.shape, NEG_INF, F32)
    acc_ref[...] = jnp.zeros(acc_ref.shape, F32)

    def chunk(c, diagonal):
        k0 = pl.multiple_of(c * tq, tq)
        for h in range(heads):
            s = jnp.dot(q_ref[:, h * dh:(h + 1) * dh], kt_ref[h, :, pl.ds(k0, tq)],
                        preferred_element_type=F32)
            s = s + nf_ref[h:h + 1, pl.ds(k0, tq)]
            if diagonal:
                row = lax.broadcasted_iota(jnp.int32, s.shape, 0)
                col = lax.broadcasted_iota(jnp.int32, s.shape, 1)
                s = jnp.where(col <= row, s, NEG_INF)
            m_prev = m_ref[h]
            m_new = jnp.maximum(m_prev, jnp.max(s, axis=1, keepdims=True))
            p = jnp.exp(s - m_new).astype(BF16)
            acc_ref[h] = jnp.exp(m_prev - m_new) * acc_ref[h] + jnp.dot(
                p, va_ref[h, pl.ds(k0, tq), :], preferred_element_type=F32)
            m_ref[h] = m_new

    def full_chunk(c, carry):
        chunk(c, False)
        return carry

    lax.fori_loop(0, qi, full_chunk, 0)
    chunk(qi, True)
    for h in range(heads):
        acc = acc_ref[h]
        o_ref[:, h * dh:(h + 1) * dh] = (acc[:, 0:dh] / acc[:, dh:2 * dh]).astype(o_ref.dtype)


def _attention(q, k, v, negf_rows, n_heads):
    b, t, d = q.shape
    dh = d // n_heads
    heads = ATTN_HEADS if n_heads % ATTN_HEADS == 0 else 1
    hw = heads * dh
    tq = _pick(t, ATTN_ROWS, LANES)
    kern = functools.partial(_attn_kernel, tq=tq, dh=dh, heads=heads)
    qspec = pl.BlockSpec((None, tq, hw), lambda bi, hp, qi: (bi, qi, hp))
    kspec = pl.BlockSpec((None, t, hw), lambda bi, hp, qi: (bi, 0, hp))
    return pl.pallas_call(
        kern,
        grid=(b, n_heads // heads, t // tq),
        in_specs=[qspec, kspec, kspec,
                  pl.BlockSpec((None, None, heads, t), lambda bi, hp, qi: (bi, hp, 0, 0))],
        out_specs=qspec,
        out_shape=jax.ShapeDtypeStruct((b, t, d), BF16),
        scratch_shapes=[pltpu.VMEM((heads, dh, t), BF16),
                        pltpu.VMEM((heads, t, 2 * dh), BF16),
                        pltpu.VMEM((heads, tq, 1), F32),
                        pltpu.VMEM((heads, tq, 2 * dh), F32)],
        compiler_params=_params(3),
        name="attention",
    )(q, k, v, negf_rows.reshape(b, n_heads // heads, heads, t))


def _page_cumsum_kernel(pt_ref, *refs, group, n_groups):
    past_refs, new_ref, o_ref, carry_ref = refs[:group], refs[group], refs[group + 1], refs[group + 2]
    p = pl.program_id(1)

    @pl.when(p == 0)
    def _():
        carry_ref[...] = jnp.zeros(carry_ref.shape, F32)

    def run(src_ref, slot):
        c = _cumsum_rows(src_ref[...]) + carry_ref[...]
        o_ref[slot] = -c
        rows = c.shape[0]
        carry_ref[...] = c[rows - 1:rows, :]

    @pl.when(p < n_groups)
    def _():
        for g in range(group):
            run(past_refs[g], g)

    @pl.when(p == n_groups)
    def _():
        run(new_ref, 0)
        for g in range(1, group):
            o_ref[g] = jnp.zeros(o_ref.shape[1:], F32)


def _page_cumsum(page_table, cache_logf, logf_new_pad, layer):
    nb, n_pages = page_table.shape
    _, _, page, h = cache_logf.shape
    group = _pick(n_pages, CUMSUM_PAGES, 1)
    n_groups = n_pages // group
    kern = functools.partial(_page_cumsum_kernel, group=group, n_groups=n_groups)

    def past_spec(g):
        return pl.BlockSpec(
            (None, None, page, h),
            lambda bi, p, pt: (layer, pt[bi * n_pages + jnp.minimum(p, n_groups - 1) * group + g], 0, 0))

    grid_spec = pltpu.PrefetchScalarGridSpec(
        num_scalar_prefetch=1,
        grid=(nb, n_groups + 1),
        in_specs=[past_spec(g) for g in range(group)]
        + [pl.BlockSpec((None, page, h), lambda bi, p, pt: (bi, 0, 0))],
        out_specs=pl.BlockSpec((None, group, page, h), lambda bi, p, pt: (bi, p, 0, 0)),
        scratch_shapes=[pltpu.VMEM((1, h), F32)])
    out = pl.pallas_call(
        kern,
        grid_spec=grid_spec,
        out_shape=jax.ShapeDtypeStruct((nb, (n_groups + 1) * group, page, h), F32),
        compiler_params=_params(2),
        name="page_cumsum",
    )(page_table.reshape(-1), *([cache_logf] * group), logf_new_pad)
    return out[:, :n_pages + 1]


def _decode_kernel(pt_ref, q_ref, *refs, group, n_groups, n_heads):
    kp_refs, vp_refs = refs[:group], refs[group:2 * group]
    kn_ref, vn_ref = refs[2 * group], refs[2 * group + 1]
    nf_refs = refs[2 * group + 2:3 * group + 2]
    o_ref, m_ref, l_ref, acc_ref = refs[3 * group + 2:]
    p = pl.program_id(1)

    @pl.when(p == 0)
    def _():
        m_ref[...] = jnp.full(m_ref.shape, NEG_INF, F32)
        l_ref[...] = jnp.zeros(l_ref.shape, F32)
        acc_ref[...] = jnp.zeros(acc_ref.shape, F32)

    def run(k_ref, v_ref, nf_ref, causal):
        s = lax.dot_general(q_ref[...], k_ref[...].astype(BF16),
                            (((1,), (1,)), ((), ())), preferred_element_type=F32)
        s = s + nf_ref[...]
        row = lax.broadcasted_iota(jnp.int32, s.shape, 0)
        col = lax.broadcasted_iota(jnp.int32, s.shape, 1)
        ok = (row % n_heads) == (col % n_heads)
        if causal:
            ok = ok & ((col // n_heads) <= (row // n_heads))
        s = jnp.where(ok, s, NEG_INF)
        _flash_step(s, v_ref[...].astype(BF16), m_ref, l_ref, acc_ref)

    @pl.when(p < n_groups)
    def _():
        for g in range(group):
            run(kp_refs[g], vp_refs[g], nf_refs[g], False)

    @pl.when(p == n_groups)
    def _():
        run(kn_ref, vn_ref, nf_refs[0], True)
        o_ref[...] = acc_ref[...] / l_ref[...]


def _decode_attention(page_table, q_rows, cache_k, cache_v, k_new, v_new, negf, layer, n_heads):
    nb, n_pages = page_table.shape
    _, qr, dh = q_rows.shape
    pr = cache_k.shape[2]
    group = _pick(n_pages, DECODE_PAGES, 1)
    n_groups = n_pages // group
    kern = functools.partial(_decode_kernel, group=group, n_groups=n_groups, n_heads=n_heads)

    def page_spec(g):
        return pl.BlockSpec(
            (None, None, pr, dh),
            lambda bi, p, pt: (layer, pt[bi * n_pages + jnp.minimum(p, n_groups - 1) * group + g], 0, 0))

    def bias_spec(g):
        return pl.BlockSpec((None, None, 1, pr),
                            lambda bi, p, pt: (bi, jnp.minimum(p * group + g, n_pages), 0, 0))

    new_spec = pl.BlockSpec((None, pr, dh), lambda bi, p, pt: (bi, 0, 0))
    pages = [page_spec(g) for g in range(group)]
    grid_spec = pltpu.PrefetchScalarGridSpec(
        num_scalar_prefetch=1,
        grid=(nb, n_groups + 1),
        in_specs=[pl.BlockSpec((None, qr, dh), lambda bi, p, pt: (bi, 0, 0))]
        + pages + pages + [new_spec, new_spec] + [bias_spec(g) for g in range(group)],
        out_specs=pl.BlockSpec((None, qr, dh), lambda bi, p, pt: (bi, 0, 0)),
        scratch_shapes=[pltpu.VMEM((qr, 1), F32), pltpu.VMEM((qr, 1), F32),
                        pltpu.VMEM((qr, dh), F32)])
    return pl.pallas_call(
        kern,
        grid_spec=grid_spec,
        out_shape=jax.ShapeDtypeStruct((nb, qr, dh), F32),
        compiler_params=_params(2),
        name="decode_attention",
    )(page_table.reshape(-1), q_rows, *([cache_k] * group), *([cache_v] * group),
      k_new, v_new, *([negf] * group))


def _forget_weights(w_qkvf, b_f, layer, d, n_heads):
    w_f = jnp.pad(w_qkvf[layer, :, 3 * d:], ((0, 0), (0, LANES - n_heads)))
    bias = jnp.pad(b_f[layer], (0, LANES - n_heads)).reshape(1, LANES)
    return w_f, bias


def _prompt_trunk(x, mod, p):
    b, t, d = x.shape
    m = b * t
    depth = p['w_up'].shape[0]
    n_heads = p['b_f'].shape[1]
    x2 = x.reshape(m, d)
    k_stack = v_stack = None
    new_logf, new_conv, new_ffn = [], [], []
    for i in range(depth):
        sh_m, sc_m, gt_m, sh_f, sc_f, gt_f = jnp.split(mod[i], 6, axis=-1)
        h = _norm_mod(x2.reshape(b, t, d), p['g_mix'][i], sh_m, sc_m).reshape(m, d)
        j = i // 2
        if i % 2 == 0:
            glu = _mm_glu(h, p['w_pw1'], j).reshape(b, t, d)
            taps = p['w_dw'].shape[1]
            new_conv.append(glu[:, t - (taps - 1):])
            y = _conv_ln(glu, p['w_dw'], p['b_dw'], p['ln_g'], p['ln_b'], j).reshape(m, d)
            x2 = _mm_res_ring(y, p['w_pw2'], j, x2, gt_m.reshape(b, 1, d), t)
        else:
            scale = (d // n_heads) ** -0.5
            wt = p['w_qkvf_t']
            q_bf, = _mm_proj(h, wt, j, 0, d, want_f32=False, want_bf16=True, bf16_scale=scale)
            k_stack, k_bf = _mm_proj(h, wt, j, d, d, prev=k_stack, want_bf16=True)
            v_stack, v_bf = _mm_proj(h, wt, j, 2 * d, d, prev=v_stack, want_bf16=True)
            w_f, bias_f = _forget_weights(p['w_qkvf'], p['b_f'], j, d, n_heads)
            logf, negf = _logf(h, w_f, bias_f, t)
            negf_rows = jnp.transpose(negf.reshape(b, t, LANES)[:, :, :n_heads], (0, 2, 1))
            o = _attention(q_bf.reshape(b, t, d), k_bf.reshape(b, t, d), v_bf.reshape(b, t, d),
                           negf_rows, n_heads).reshape(m, d)
            x2 = _mm_res_ring(o, p['w_o'], j, x2, gt_m.reshape(b, 1, d), t)
            new_logf.append(logf.reshape(b, t, LANES)[:, :, :n_heads])
        h = _norm_mod(x2.reshape(b, t, d), p['g_ffn'][i], sh_f, sc_f).reshape(m, d)
        act, st_a, st_b = _mm_ffn(h, p['w_up'], p['w_ffn_dw'], p['b_ffn_dw'], i, t)
        new_ffn.append(jnp.concatenate([st_a, st_b], axis=-1))
        x2 = _mm_res_bf16w(act, p['w_down_bf'], i, x2, gt_f.reshape(b, 1, d), t)
    y = _final_norm(x2.reshape(b, t, d), p['g_out'])
    heads_shape = (-1, b, t, n_heads, d // n_heads)
    return (y, k_stack.reshape(heads_shape), v_stack.reshape(heads_shape), jnp.stack(new_logf),
            jnp.stack(new_conv), jnp.stack(new_ffn))


def _sample_trunk(x, mod, p, cache_k, cache_v, cache_logf, state_conv, state_ffn, page_table):
    b, t, d = x.shape
    m = b * t
    depth = p['w_up'].shape[0]
    n_heads = p['b_f'].shape[1]
    dh = d // n_heads
    nl, n_pool, page, _, _ = cache_k.shape
    ck = cache_k.reshape(nl, n_pool, page * n_heads, dh)
    cv = cache_v.reshape(nl, n_pool, page * n_heads, dh)
    x2 = x.reshape(m, d)
    rep = lambda g: jnp.repeat(g, t, axis=0).reshape(1, m, d)
    pad_page = lambda a: jnp.pad(a, ((0, 0), (0, page - t)) + ((0, 0),) * (a.ndim - 2))
    new_k, new_v, new_logf, new_conv, new_ffn = [], [], [], [], []
    for i in range(depth):
        sh_m, sc_m, gt_m, sh_f, sc_f, gt_f = jnp.split(mod[i], 6, axis=-1)
        h = _norm_mod(x2.reshape(b, t, d), p['g_mix'][i], sh_m, sc_m).reshape(m, d)
        j = i // 2
        if i % 2 == 0:
            glu = _mm_glu(h, p['w_pw1'], j).reshape(b, t, d)
            xp = jnp.concatenate([state_conv[j], glu], axis=1)
            taps = p['w_dw'].shape[1]
            new_conv.append(xp[:, xp.shape[1] - (taps - 1):])
            y = _conv_small(xp, p['w_dw'], p['b_dw'], p['ln_g'], p['ln_b'], j).reshape(m, d)
            x2 = _mm_res(y, p['w_pw2'], j, x2, rep(gt_m), m)
        else:
            wt = p['w_qkvf_t']
            q, = _mm_proj(h, wt, j, 0, d, want_f32=False, want_bf16=True, bf16_scale=dh ** -0.5)
            k = _mm_proj(h, wt, j, d, d)[0][0]
            v = _mm_proj(h, wt, j, 2 * d, d)[0][0]
            w_f, bias_f = _forget_weights(p['w_qkvf'], p['b_f'], j, d, n_heads)
            logf = _logf(h, w_f, bias_f, m)[0].reshape(b, t, LANES)[:, :, :n_heads]
            negf = _page_cumsum(page_table, cache_logf, pad_page(logf), j)
            k_new = pad_page(k.reshape(b, t, n_heads, dh)).reshape(b, page * n_heads, dh)
            v_new = pad_page(v.reshape(b, t, n_heads, dh)).reshape(b, page * n_heads, dh)
            o = _decode_attention(
                page_table, q.reshape(b, t * n_heads, dh), ck, cv, k_new, v_new,
                negf.reshape(b, negf.shape[1], 1, page * n_heads), j, n_heads)
            x2 = _mm_res(o.reshape(m, d).astype(BF16), p['w_o'], j, x2, rep(gt_m), m)
            new_k.append(k.reshape(b, t, n_heads, dh))
            new_v.append(v.reshape(b, t, n_heads, dh))
            new_logf.append(logf)
        h = _norm_mod(x2.reshape(b, t, d), p['g_ffn'][i], sh_f, sc_f).reshape(m, d)
        f2 = p['w_up'].shape[2]
        u = _mm_plain(h, p['w_up'], i, 0, f2)[0].reshape(b, t, f2)
        xp = jnp.concatenate([state_ffn[i], u], axis=1)
        ftaps = p['w_ffn_dw'].shape[1]
        new_ffn.append(xp[:, xp.shape[1] - (ftaps - 1):])
        act = _ffn_small(xp, p['w_ffn_dw'], p['b_ffn_dw'], i).reshape(m, f2 // 2)
        x2 = _mm_res_bf16w(act, p['w_down_bf'], i, x2, rep(gt_f), m)
    y = _final_norm(x2.reshape(b, t, d), p['g_out'])
    return (y, jnp.stack(new_k), jnp.stack(new_v), jnp.stack(new_logf),
            jnp.stack(new_conv), jnp.stack(new_ffn))


def kernel(x_prompt, x_sample, c_prompt, c_sample, cache_k, cache_v, cache_logf, state_conv, state_ffn, page_table, w_ada, b_ada, g_mix, g_ffn, g_out, w_pw1, w_dw, b_dw, ln_g, ln_b, w_pw2, w_qkvf, b_f, w_o, w_up, w_ffn_dw, b_ffn_dw, w_down):
    p = dict(g_mix=g_mix, g_ffn=g_ffn, g_out=g_out, w_pw1=w_pw1, w_dw=w_dw, b_dw=b_dw,
             ln_g=ln_g, ln_b=ln_b, w_pw2=w_pw2, w_qkvf=w_qkvf,
             w_qkvf_t=jnp.swapaxes(w_qkvf, 1, 2), b_f=b_f, w_o=w_o,
             w_up=w_up, w_ffn_dw=w_ffn_dw, b_ffn_dw=b_ffn_dw, w_down_bf=_cast_bf16(w_down))
    nb_p = c_prompt.shape[0]
    nb_s = c_sample.shape[0]
    c_all = jnp.concatenate([c_prompt, c_sample], axis=0)
    pad = (-c_all.shape[0]) % 16
    c_all = jnp.pad(c_all, ((0, pad), (0, 0)))
    mod = _ada(c_all, w_ada, b_ada)
    outs_p = _prompt_trunk(x_prompt, mod[:, :nb_p], p)
    outs_s = _sample_trunk(x_sample, mod[:, nb_p:nb_p + nb_s], p, cache_k, cache_v,
                           cache_logf, state_conv, state_ffn, page_table)
    return (outs_p[0], outs_s[0]) + outs_p[1:] + outs_s[1:]
```
